```python
import jax, jax.numpy as jnp
from jax import lax
import numpy as np

D_MODEL = 1024
BATCH = 8
SEQ = 8192
DEPTH = 1

CHUNK = 64
DN_HEADS = 8
DN_HEAD_DIM = 128
DN_WIDTH = DN_HEADS * DN_HEAD_DIM
DN_CONV = 4
CM_WIDTH = D_MODEL
CM_KERNEL = 31
N_BRANCH = 2
FFN_HIDDEN = -(-8 * D_MODEL // (3 * 256)) * 256
EPS = 1e-6
SPLITS = (DN_WIDTH, 2 * DN_WIDTH, 3 * DN_WIDTH, 4 * DN_WIDTH,
          4 * DN_WIDTH + DN_HEADS, 4 * DN_WIDTH + 2 * DN_HEADS,
          4 * DN_WIDTH + 2 * DN_HEADS + 2 * CM_WIDTH)
IN_COLS = 4 * DN_WIDTH + 2 * DN_HEADS + 2 * CM_WIDTH + N_BRANCH * D_MODEL

kernel_name = "hybrid_gdn_conformer_gated_block"


def rmsnorm(x, w):
    x32 = x.astype(jnp.float32)
    y = x32 * lax.rsqrt(jnp.mean(x32 * x32, axis=-1, keepdims=True) + EPS)
    return y.astype(x.dtype) * w


def layernorm(x, w, b):
    x32 = x.astype(jnp.float32)
    mu = jnp.mean(x32, axis=-1, keepdims=True)
    xc = x32 - mu
    y = xc * lax.rsqrt(jnp.mean(xc * xc, axis=-1, keepdims=True) + EPS)
    return y.astype(x.dtype) * w + b


def l2norm(x):
    return x * lax.rsqrt(jnp.sum(x * x, axis=-1, keepdims=True) + EPS)


def causal_depthwise_conv(x, w):
    K = w.shape[0]
    return lax.conv_general_dilated(
        x, w[:, None, :], window_strides=(1,), padding=[(K - 1, 0)],
        dimension_numbers=("NWC", "WIO", "NWC"), feature_group_count=x.shape[-1])


def gated_delta_rule_chunked(q, k, v, g, beta):
    q, k, v, g, beta = (t.astype(jnp.float32) for t in (q, k, v, g, beta))
    B, T, H, DK = q.shape
    DV = v.shape[-1]
    N = T // CHUNK
    chunk = lambda t: jnp.moveaxis(t.reshape(B, N, CHUNK, H, *t.shape[3:]), 3, 2)
    q = chunk(q) * (DK ** -0.5)
    k, v, g, beta = chunk(k), chunk(v), chunk(g), chunk(beta)
    gc = jnp.cumsum(g, axis=-1)
    idx = jnp.arange(CHUNK)
    tril = idx[:, None] >= idx[None, :]
    strict = idx[:, None] > idx[None, :]
    decay = jnp.exp(jnp.where(tril, gc[..., :, None] - gc[..., None, :], -jnp.inf))
    kk = jnp.einsum("bnhid,bnhjd->bnhij", k, k)
    m_low = jnp.where(strict, beta[..., :, None] * kk * decay, 0.0)
    t_mat = m_low + jnp.eye(CHUNK, dtype=jnp.float32)
    rhs = jnp.concatenate([v * beta[..., None], k * (beta * jnp.exp(gc))[..., None]], axis=-1)
    sol = lax.linalg.triangular_solve(t_mat, rhs, left_side=True, lower=True, unit_diagonal=True)
    u, w = sol[..., :DV], sol[..., DV:]
    attn = jnp.einsum("bnhid,bnhjd->bnhij", q, k) * decay
    q_dec = q * jnp.exp(gc)[..., None]
    g_last = gc[..., -1]
    k_dec = k * jnp.exp(g_last[..., None] - gc)[..., None]
    xs = tuple(jnp.moveaxis(t, 1, 0) for t in (u, w, attn, q_dec, k_dec, jnp.exp(g_last)))

    def step(S, xs_c):
        u_c, w_c, a_c, qd_c, kd_c, el_c = xs_c
        v_new = u_c - jnp.einsum("bhcd,bhde->bhce", w_c, S)
        o_c = jnp.einsum("bhcd,bhde->bhce", qd_c, S) + jnp.einsum("bhij,bhje->bhie", a_c, v_new)
        S = S * el_c[..., None, None] + jnp.einsum("bhcd,bhce->bhde", kd_c, v_new)
        return S, o_c

    S0 = jnp.zeros((B, H, DK, DV), jnp.float32)
    _, o = lax.scan(step, S0, xs)
    o = jnp.moveaxis(o, 0, 1)
    return jnp.moveaxis(o, 2, 3).reshape(B, T, H, DV)


def setup_inputs(seed: int = 0) -> dict:
    key = jax.random.key(seed)
    ks = jax.random.split(key, 24)
    L = DEPTH
    nrm = lambda k, shape, fan_in: jax.random.normal(k, shape, jnp.float32) * (fan_in ** -0.5)
    gain = lambda k, shape: 1.0 + 0.02 * jax.random.normal(k, shape, jnp.float32)
    small = lambda k, shape: 0.01 * jax.random.normal(k, shape, jnp.float32)
    dt = jnp.exp(jax.random.uniform(ks[7], (L, DN_HEADS), jnp.float32, np.log(1e-3), np.log(1e-1)))
    return {
        "x": jax.random.normal(ks[0], (BATCH, SEQ, D_MODEL), jnp.float32),
        "norm1_w": gain(ks[1], (L, D_MODEL)),
        "w_in": nrm(ks[2], (L, D_MODEL, IN_COLS), D_MODEL),
        "b_glu": small(ks[3], (L, 2 * CM_WIDTH)),
        "b_gates": small(ks[4], (L, N_BRANCH * D_MODEL)),
        "dn_conv_w": nrm(ks[5], (L, DN_CONV, 3 * DN_WIDTH), DN_CONV),
        "dn_A_log": jnp.log(jax.random.uniform(ks[6], (L, DN_HEADS), jnp.float32, 1.0, 16.0)),
        "dn_dt_bias": dt + jnp.log(-jnp.expm1(-dt)),
        "dn_norm_w": gain(ks[8], (L, DN_HEAD_DIM)),
        "dn_w_o": nrm(ks[9], (L, DN_WIDTH, D_MODEL), DN_WIDTH),
        "cm_dw_w": nrm(ks[10], (L, CM_KERNEL, CM_WIDTH), CM_KERNEL),
        "cm_dw_b": small(ks[11], (L, CM_WIDTH)),
        "cm_ln_w": gain(ks[12], (L, CM_WIDTH)),
        "cm_ln_b": small(ks[13], (L, CM_WIDTH)),
        "cm_w_pw2": nrm(ks[14], (L, CM_WIDTH, D_MODEL), CM_WIDTH),
        "cm_b_pw2": small(ks[15], (L, D_MODEL)),
        "w_out": nrm(ks[16], (L, D_MODEL, D_MODEL), D_MODEL),
        "norm2_w": gain(ks[17], (L, D_MODEL)),
        "ffn_w_gate_up": nrm(ks[18], (L, D_MODEL, 2 * FFN_HIDDEN), D_MODEL),
        "ffn_w_down": nrm(ks[19], (L, FFN_HIDDEN, D_MODEL), FFN_HIDDEN),
        "norm_f_w": gain(ks[20], (D_MODEL,)),
    }


def reference(x, norm1_w, w_in, b_glu, b_gates, dn_conv_w, dn_A_log, dn_dt_bias, dn_norm_w,
              dn_w_o, cm_dw_w, cm_dw_b, cm_ln_w, cm_ln_b, cm_w_pw2, cm_b_pw2, w_out,
              norm2_w, ffn_w_gate_up, ffn_w_down, norm_f_w):
    B, T, _ = x.shape
    for l in range(DEPTH):
        h = rmsnorm(x, norm1_w[l])
        proj = h @ w_in[l]
        q, k, v, z, b_raw, a_raw, glu, gates = jnp.split(proj, SPLITS, axis=-1)

        qkv = jax.nn.silu(causal_depthwise_conv(jnp.concatenate([q, k, v], axis=-1), dn_conv_w[l]))
        q, k, v = (t.reshape(B, T, DN_HEADS, DN_HEAD_DIM) for t in jnp.split(qkv, 3, axis=-1))
        q, k = l2norm(q.astype(jnp.float32)), l2norm(k.astype(jnp.float32))
        beta = jax.nn.sigmoid(b_raw.astype(jnp.float32))
        g = -jnp.exp(dn_A_log[l].astype(jnp.float32)) * jax.nn.softplus(
            a_raw.astype(jnp.float32) + dn_dt_bias[l].astype(jnp.float32))
        o = gated_delta_rule_chunked(q, k, v, g, beta).astype(x.dtype)
        o = rmsnorm(o, dn_norm_w[l]) * jax.nn.silu(z.reshape(B, T, DN_HEADS, DN_HEAD_DIM))
        out_a = o.reshape(B, T, DN_WIDTH) @ dn_w_o[l]

        ga, gb = jnp.split(glu + b_glu[l], 2, axis=-1)
        c = ga * jax.nn.sigmoid(gb)
        c = causal_depthwise_conv(c, cm_dw_w[l]) + cm_dw_b[l]
        c = jax.nn.silu(layernorm(c, cm_ln_w[l], cm_ln_b[l]))
        out_b = c @ cm_w_pw2[l] + cm_b_pw2[l]

        gate_a, gate_b = jnp.split(jax.nn.sigmoid(gates + b_gates[l]), 2, axis=-1)
        x = x + (gate_a * out_a + gate_b * out_b) @ w_out[l]

        h2 = rmsnorm(x, norm2_w[l])
        f_gate, f_up = jnp.split(h2 @ ffn_w_gate_up[l], 2, axis=-1)
        x = x + (jax.nn.silu(f_gate) * f_up) @ ffn_w_down[l]
    return rmsnorm(x, norm_f_w)
```

```python
import functools

import jax
import jax.numpy as jnp
from jax import lax
from jax.experimental import pallas as pl
from jax.experimental.pallas import tpu as pltpu

F32 = jnp.float32
BF16 = jnp.bfloat16

EPS = 1e-6
CHUNK = 64
HEADS = 8
HEAD_DIM = 128
DN_CONV = 4
CM_KERNEL = 31
LANES = 128
SUBLANES = 8
VMEM_LIMIT_BYTES = 56 * 1024 * 1024


def _mm(a, b):
    return jnp.dot(a.astype(BF16), b.astype(BF16), preferred_element_type=F32)


def _mm_nt(a, b):
    return lax.dot_general(a.astype(BF16), b.astype(BF16), (((1,), (1,)), ((), ())),
                           preferred_element_type=F32)


def _mm_tn(a, b):
    return lax.dot_general(a.astype(BF16), b.astype(BF16), (((0,), (0,)), ((), ())),
                           preferred_element_type=F32)


def _split(a):
    hi = a.astype(BF16)
    lo = (a - hi.astype(F32)).astype(BF16)
    return hi, lo


def _mm_hi(a, b):
    ah, al = _split(a)
    bh, bl = _split(b)
    d = functools.partial(jnp.dot, preferred_element_type=F32)
    return d(ah, bh) + (d(ah, bl) + d(al, bh))


def _sigmoid(x):
    return 1.0 / (1.0 + jnp.exp(-x))


def _silu(x):
    return x * _sigmoid(x)


def _inproj_kernel(x_ref, nw_ref, w_ref, wba_ref, p_ref, ba_ref, h_ref):
    j = pl.program_id(1)

    @pl.when(j == 0)
    def _():
        x = x_ref[...]
        ms = jnp.mean(x * x, axis=-1, keepdims=True)
        h = x * lax.rsqrt(ms + EPS) * nw_ref[...]
        hh, hl = _split(h)
        h_ref[...] = hh
        r = jnp.dot(hh, wba_ref[...], preferred_element_type=F32)
        r2 = jnp.dot(hl, wba_ref[:, :LANES], preferred_element_type=F32)
        ba_ref[...] = r[:, :LANES] + (r[:, LANES:] + r2)

    p_ref[...] = jnp.dot(h_ref[...], w_ref[...], preferred_element_type=F32).astype(p_ref.dtype)


def _inproj(x2, norm_w, w_main, w_ba, *, tm, tn):
    n, d = x2.shape
    cols = w_main.shape[1]
    return pl.pallas_call(
        _inproj_kernel,
        grid=(n // tm, cols // tn),
        in_specs=[
            pl.BlockSpec((tm, d), lambda i, j: (i, 0)),
            pl.BlockSpec((1, d), lambda i, j: (0, 0)),
            pl.BlockSpec((d, tn), lambda i, j: (0, j)),
            pl.BlockSpec((d, 2 * LANES), lambda i, j: (0, 0)),
        ],
        out_specs=[
            pl.BlockSpec((tm, tn), lambda i, j: (i, j)),
            pl.BlockSpec((tm, LANES), lambda i, j: (i, 0)),
        ],
        out_shape=[
            jax.ShapeDtypeStruct((n, cols), BF16),
            jax.ShapeDtypeStruct((n, LANES), F32),
        ],
        scratch_shapes=[pltpu.VMEM((tm, d), BF16)],
        compiler_params=pltpu.CompilerParams(
            dimension_semantics=("arbitrary", "arbitrary"),
            vmem_limit_bytes=VMEM_LIMIT_BYTES),
        name="inproj",
    )(x2, norm_w, w_main, w_ba)


def _delta_kernel(qkv_ref, z_ref, ba_ref, cw_ref, alog_ref, dtb_ref, nw_ref, wo_ref, out_ref,
                  xh_ref, act_ref, g_ref, gt_ref, s_ref, o_ref, *, tt):
    width = HEADS * HEAD_DIM
    nchunk = tt // CHUNK
    nslab = 3 * HEADS
    hist = SUBLANES
    t = pl.program_id(1)

    @pl.when(t == 0)
    def _():
        xh_ref[:, 0:hist, :] = jnp.zeros((nslab, hist, HEAD_DIM), F32)
        s_ref[...] = jnp.zeros_like(s_ref)

    for j in range(nslab):
        xh_ref[j, hist:hist + tt, :] = qkv_ref[:, j * HEAD_DIM:(j + 1) * HEAD_DIM].astype(F32)

    rb = 4 * SUBLANES

    def conv_body(j, carry):
        is_qk = j < 2 * HEADS
        post = jnp.where(j < HEADS, HEAD_DIM ** -0.5, 1.0).astype(F32)
        for r0 in range(0, tt, rb):
            acc = None
            for k in range(DN_CONV):
                lo = r0 + hist - (DN_CONV - 1) + k
                term = xh_ref[j, lo:lo + rb, :] * cw_ref[j, k:k + 1, :]
                acc = term if acc is None else acc + term
            y = _silu(acc)
            inv = lax.rsqrt(jnp.sum(y * y, axis=-1, keepdims=True) + EPS)
            act_ref[j, r0:r0 + rb, :] = y * (jnp.where(is_qk, inv, 1.0) * post)
        return carry

    lax.fori_loop(0, nslab, conv_body, 0)
    xh_ref[:, 0:hist, :] = xh_ref[:, tt:tt + hist, :]

    ba = ba_ref[...]
    lane = lax.broadcasted_iota(jnp.int32, (tt, LANES), 1)
    beta = _sigmoid(ba)
    sp_in = ba + dtb_ref[...]
    softplus = jnp.maximum(sp_in, 0.0) + jnp.log1p(jnp.exp(-jnp.abs(sp_in)))
    g = -jnp.exp(alog_ref[...]) * softplus
    g = jnp.where((lane >= HEADS) & (lane < 2 * HEADS), g, 0.0)
    row = lax.broadcasted_iota(jnp.int32, (tt, tt), 0)
    col = lax.broadcasted_iota(jnp.int32, (tt, tt), 1)
    tri = jnp.where((row // CHUNK == col // CHUNK) & (row >= col), 1.0, 0.0).astype(BF16)
    gh, gl = _split(g)
    gc = jnp.dot(tri, gh, preferred_element_type=F32) + jnp.dot(tri, gl, preferred_element_type=F32)
    gfull = jnp.where(lane < HEADS, beta, gc)
    g_ref[...] = gfull
    gt = gfull.T
    for c in range(nchunk):
        gt_ref[c] = gt[:, c * CHUNK:(c + 1) * CHUNK]

    ri = lax.broadcasted_iota(jnp.int32, (CHUNK, CHUNK), 0)
    ci = lax.broadcasted_iota(jnp.int32, (CHUNK, CHUNK), 1)
    tril = ri >= ci
    eye = jnp.where(ri == ci, 1.0, 0.0).astype(F32)
    level_masks = []
    s = 1
    while s < CHUNK:
        level_masks.append((ri // (2 * s) == ci // (2 * s)) & ((ri // s) % 2 == 1) & ((ci // s) % 2 == 0))
        s *= 2

    def chunk_body(c, carry):
        r0 = pl.multiple_of(c * CHUNK, CHUNK)
        gcb = g_ref[pl.ds(r0, CHUNK), :]
        gtb = gt_ref[c]
        for h in range(HEADS):
            beta_c = gcb[:, h:h + 1]
            gc_c = gcb[:, HEADS + h:HEADS + h + 1]
            gc_r = gtb[HEADS + h:HEADS + h + 1, :]
            q = act_ref[h, pl.ds(r0, CHUNK), :]
            k = act_ref[HEADS + h, pl.ds(r0, CHUNK), :]
            v = act_ref[2 * HEADS + h, pl.ds(r0, CHUNK), :]

            decay = jnp.exp(jnp.where(tril, gc_c - gc_r, -jnp.inf))
            kq = _mm_nt(jnp.concatenate([k, q], axis=0), k)
            m = beta_c * kq[:CHUNK] * decay
            attn = kq[CHUNK:] * decay

            x = eye - jnp.where(level_masks[0], m, 0.0)
            for lm in level_masks[1:]:
                x = x - _mm_hi(x, _mm_hi(jnp.where(lm, m, 0.0), x))

            egc = jnp.exp(gc_c)
            rhs = jnp.concatenate([v * beta_c, k * (beta_c * egc)], axis=1)
            uw = _mm_hi(x, rhs)
            u = uw[:, :HEAD_DIM]
            w = uw[:, HEAD_DIM:]

            st = s_ref[h]
            r = _mm(jnp.concatenate([w, q * egc], axis=0), st)
            v_new = u - r[:CHUNK]
            g_last = gc_c[CHUNK - 1:CHUNK, :]
            k_dec = k * jnp.exp(g_last - gc_c)
            o_ref[pl.ds(r0, CHUNK), h * HEAD_DIM:(h + 1) * HEAD_DIM] = r[CHUNK:] + _mm(attn, v_new)
            s_ref[h] = st * jnp.exp(g_last) + _mm_tn(k_dec, v_new)
        return carry

    lax.fori_loop(0, nchunk, chunk_body, 0)

    parts = []
    for h in range(HEADS):
        o = o_ref[:, h * HEAD_DIM:(h + 1) * HEAD_DIM]
        z = z_ref[:, h * HEAD_DIM:(h + 1) * HEAD_DIM].astype(F32)
        on = o * lax.rsqrt(jnp.mean(o * o, axis=-1, keepdims=True) + EPS) * nw_ref[...]
        parts.append((on * _silu(z)).astype(BF16))
    og = jnp.concatenate(parts, axis=1)
    out_ref[...] = jnp.dot(og, wo_ref[...], preferred_element_type=F32).astype(out_ref.dtype)


def _delta_branch(p, ba, conv_w, alog, dtb, norm_w, w_o, *, batch, seq, tt):
    width = HEADS * HEAD_DIM
    nt = seq // tt
    d_out = w_o.shape[1]
    kern = functools.partial(_delta_kernel, tt=tt)
    row = lambda b, t: b * nt + t
    return pl.pallas_call(
        kern,
        grid=(batch, nt),
        in_specs=[
            pl.BlockSpec((tt, 3 * width), lambda b, t: (row(b, t), 0)),
            pl.BlockSpec((tt, width), lambda b, t: (row(b, t), 3)),
            pl.BlockSpec((tt, LANES), lambda b, t: (row(b, t), 0)),
            pl.BlockSpec((3 * HEADS, SUBLANES, HEAD_DIM), lambda b, t: (0, 0, 0)),
            pl.BlockSpec((1, LANES), lambda b, t: (0, 0)),
            pl.BlockSpec((1, LANES), lambda b, t: (0, 0)),
            pl.BlockSpec((1, HEAD_DIM), lambda b, t: (0, 0)),
            pl.BlockSpec((width, d_out), lambda b, t: (0, 0)),
        ],
        out_specs=pl.BlockSpec((tt, d_out), lambda b, t: (row(b, t), 0)),
        out_shape=jax.ShapeDtypeStruct((batch * seq, d_out), BF16),
        scratch_shapes=[
            pltpu.VMEM((3 * HEADS, SUBLANES + tt, HEAD_DIM), F32),
            pltpu.VMEM((3 * HEADS, tt, HEAD_DIM), F32),
            pltpu.VMEM((tt, LANES), F32),
            pltpu.VMEM((tt // CHUNK, LANES, CHUNK), F32),
            pltpu.VMEM((HEADS, HEAD_DIM, HEAD_DIM), F32),
            pltpu.VMEM((tt, width), F32),
        ],
        compiler_params=pltpu.CompilerParams(
            dimension_semantics=("arbitrary", "arbitrary"),
            vmem_limit_bytes=VMEM_LIMIT_BYTES),
        name="delta_branch",
    )(p, p, ba, conv_w, alog, dtb, norm_w, w_o)


def _convmod_kernel(glu_ref, bglu_ref, dw_ref, lnw_ref, lnb_ref, w2_ref, b2_ref, out_ref,
                    xh_ref, y_ref, c_ref, *, tt):
    width = lnw_ref.shape[1]
    nslab = width // LANES
    hist = 4 * SUBLANES
    t = pl.program_id(1)

    @pl.when(t == 0)
    def _():
        xh_ref[:, 0:hist, :] = jnp.zeros((nslab, hist, LANES), F32)

    for j in range(nslab):
        ca = slice(j * LANES, (j + 1) * LANES)
        cb = slice(width + j * LANES, width + (j + 1) * LANES)
        ga = glu_ref[:, ca].astype(F32) + bglu_ref[:, ca]
        gb = glu_ref[:, cb].astype(F32) + bglu_ref[:, cb]
        xh_ref[j, hist:hist + tt, :] = ga * _sigmoid(gb)

    rb = 4 * SUBLANES

    def conv_body(j, carry):
        for r0 in range(0, tt, rb):
            acc = None
            for k in range(CM_KERNEL):
                lo = r0 + hist - (CM_KERNEL - 1) + k
                term = xh_ref[j, lo:lo + rb, :] * dw_ref[j, k:k + 1, :]
                acc = term if acc is None else acc + term
            y_ref[j, r0:r0 + rb, :] = acc + dw_ref[j, CM_KERNEL:CM_KERNEL + 1, :]
        return carry

    lax.fori_loop(0, nslab, conv_body, 0)
    xh_ref[:, 0:hist, :] = xh_ref[:, tt:tt + hist, :]

    tot = y_ref[0]
    for j in range(1, nslab):
        tot = tot + y_ref[j]
    mu = jnp.sum(tot, axis=-1, keepdims=True) * (1.0 / width)
    sq = None
    for j in range(nslab):
        cc = y_ref[j] - mu
        sq = cc * cc if sq is None else sq + cc * cc
    rstd = lax.rsqrt(jnp.sum(sq, axis=-1, keepdims=True) * (1.0 / width) + EPS)
    for j in range(nslab):
        ca = slice(j * LANES, (j + 1) * LANES)
        y = (y_ref[j] - mu) * rstd * lnw_ref[:, ca] + lnb_ref[:, ca]
        c_ref[:, ca] = _silu(y).astype(BF16)

    out = jnp.dot(c_ref[...], w2_ref[...], preferred_element_type=F32) + b2_ref[...]
    out_ref[...] = out.astype(out_ref.dtype)


def _conv_branch(p, b_glu, dw_slabs, ln_w, ln_b, w_pw2, b_pw2, *, batch, seq, tt):
    width = ln_w.shape[1]
    nslab = width // LANES
    d_out = w_pw2.shape[1]
    nt = seq // tt
    kern = functools.partial(_convmod_kernel, tt=tt)
    row = lambda b, t: b * nt + t
    const = lambda b, t: (0, 0)
    return pl.pallas_call(
        kern,
        grid=(batch, nt),
        in_specs=[
            pl.BlockSpec((tt, 2 * width), lambda b, t: (row(b, t), 2)),
            pl.BlockSpec((1, 2 * width), const),
            pl.BlockSpec((nslab, CM_KERNEL + 1, LANES), lambda b, t: (0, 0, 0)),
            pl.BlockSpec((1, width), const),
            pl.BlockSpec((1, width), const),
            pl.BlockSpec((width, d_out), const),
            pl.BlockSpec((1, d_out), const),
        ],
        out_specs=pl.BlockSpec((tt, d_out), lambda b, t: (row(b, t), 0)),
        out_shape=jax.ShapeDtypeStruct((batch * seq, d_out), BF16),
        scratch_shapes=[
            pltpu.VMEM((nslab, 4 * SUBLANES + tt, LANES), F32),
            pltpu.VMEM((nslab, tt, LANES), F32),
            pltpu.VMEM((tt, width), BF16),
        ],
        compiler_params=pltpu.CompilerParams(
            dimension_semantics=("arbitrary", "arbitrary"),
            vmem_limit_bytes=VMEM_LIMIT_BYTES),
        name="conv_branch",
    )(p, b_glu, dw_slabs, ln_w, ln_b, w_pw2, b_pw2)


def _merge_ffn_kernel(x_ref, oa_ref, ob_ref, gates_ref, bg_ref, wout_ref, n2_ref, wgu_ref, wdn_ref,
                      nf_ref, out_ref):
    d = x_ref.shape[1]
    hidden = wdn_ref.shape[0]
    gate_a = _sigmoid(gates_ref[:, :d].astype(F32) + bg_ref[:, :d])
    gate_b = _sigmoid(gates_ref[:, d:].astype(F32) + bg_ref[:, d:])
    merged = gate_a * oa_ref[...].astype(F32) + gate_b * ob_ref[...].astype(F32)
    x1 = x_ref[...] + jnp.dot(merged.astype(BF16), wout_ref[...], preferred_element_type=F32)
    h2 = x1 * lax.rsqrt(jnp.mean(x1 * x1, axis=-1, keepdims=True) + EPS) * n2_ref[...]
    gu = jnp.dot(h2.astype(BF16), wgu_ref[...], preferred_element_type=F32)
    act = (_silu(gu[:, :hidden]) * gu[:, hidden:]).astype(BF16)
    x2 = x1 + jnp.dot(act, wdn_ref[...], preferred_element_type=F32)
    out_ref[...] = x2 * lax.rsqrt(jnp.mean(x2 * x2, axis=-1, keepdims=True) + EPS) * nf_ref[...]


def _merge_ffn(x2d, out_a, out_b, p, b_gates, w_out, norm2_w, w_gu, w_dn, norm_f_w, *, tm):
    n, d = x2d.shape
    hidden = w_dn.shape[0]
    const = lambda i: (0, 0)
    single = dict(pipeline_mode=pl.Buffered(1))
    return pl.pallas_call(
        _merge_ffn_kernel,
        grid=(n // tm,),
        in_specs=[
            pl.BlockSpec((tm, d), lambda i: (i, 0)),
            pl.BlockSpec((tm, d), lambda i: (i, 0)),
            pl.BlockSpec((tm, d), lambda i: (i, 0)),
            pl.BlockSpec((tm, 2 * d), lambda i: (i, 3)),
            pl.BlockSpec((1, 2 * d), const),
            pl.BlockSpec((d, d), const, **single),
            pl.BlockSpec((1, d), const),
            pl.BlockSpec((d, 2 * hidden), const, **single),
            pl.BlockSpec((hidden, d), const, **single),
            pl.BlockSpec((1, d), const),
        ],
        out_specs=pl.BlockSpec((tm, d), lambda i: (i, 0)),
        out_shape=jax.ShapeDtypeStruct((n, d), F32),
        compiler_params=pltpu.CompilerParams(
            dimension_semantics=("arbitrary",),
            vmem_limit_bytes=VMEM_LIMIT_BYTES),
        name="merge_ffn",
    )(x2d, out_a, out_b, p, b_gates, w_out, norm2_w, w_gu, w_dn, norm_f_w)


def _pick_tile(n, target):
    t = min(n, target)
    while n % t:
        t //= 2
    return t


def kernel(x, norm1_w, w_in, b_glu, b_gates, dn_conv_w, dn_A_log, dn_dt_bias, dn_norm_w, dn_w_o,
           cm_dw_w, cm_dw_b, cm_ln_w, cm_ln_b, cm_w_pw2, cm_b_pw2, w_out, norm2_w, ffn_w_gate_up,
           ffn_w_down, norm_f_w):
    batch, seq, d = x.shape
    depth = w_in.shape[0]
    width = HEADS * HEAD_DIM
    assert d == width and cm_ln_w.shape[1] == d and seq % CHUNK == 0
    n = batch * seq
    tt = _pick_tile(seq, 256)
    xf = x.reshape(n, d)

    for l in range(depth):
        wl = w_in[l]
        w_main = jnp.concatenate([wl[:, :4 * width], wl[:, 4 * width + 2 * HEADS:]], axis=1).astype(BF16)
        w_ba = jnp.pad(wl[:, 4 * width:4 * width + 2 * HEADS], ((0, 0), (0, LANES - 2 * HEADS)))
        w_ba_hi = w_ba.astype(BF16)
        w_ba_lo = (w_ba - w_ba_hi.astype(F32)).astype(BF16)
        w_ba2 = jnp.concatenate([w_ba_hi, w_ba_lo], axis=1)
        lane_pad = lambda v: jnp.pad(v.reshape(1, HEADS), ((0, 0), (HEADS, LANES - 2 * HEADS)))

        p, ba = _inproj(xf, norm1_w[l].reshape(1, d), w_main, w_ba2,
                        tm=_pick_tile(n, 1024), tn=1024)
        dn_cw = jnp.pad(dn_conv_w[l].reshape(DN_CONV, 3 * HEADS, HEAD_DIM).transpose(1, 0, 2),
                        ((0, 0), (0, SUBLANES - DN_CONV), (0, 0)))
        cm_cw = jnp.concatenate([cm_dw_w[l], cm_dw_b[l][None, :]], axis=0)
        cm_cw = cm_cw.reshape(CM_KERNEL + 1, d // LANES, LANES).transpose(1, 0, 2)

        out_a = _delta_branch(p, ba, dn_cw, lane_pad(dn_A_log[l]), lane_pad(dn_dt_bias[l]),
                              dn_norm_w[l].reshape(1, HEAD_DIM), dn_w_o[l].astype(BF16),
                              batch=batch, seq=seq, tt=tt)
        out_b = _conv_branch(p, b_glu[l].reshape(1, 2 * d), cm_cw,
                             cm_ln_w[l].reshape(1, d), cm_ln_b[l].reshape(1, d),
                             cm_w_pw2[l].astype(BF16), cm_b_pw2[l].reshape(1, d),
                             batch=batch, seq=seq, tt=tt)
        assert depth == 1
        xf = _merge_ffn(xf, out_a, out_b, p, b_gates[l].reshape(1, 2 * d), w_out[l].astype(BF16),
                        norm2_w[l].reshape(1, d), ffn_w_gate_up[l].astype(BF16),
                        ffn_w_down[l].astype(BF16), norm_f_w.reshape(1, d), tm=_pick_tile(n, 256))
    return xf.reshape(batch, seq, d)
```

```python
import functools

import jax
import jax.numpy as jnp
from jax import lax
from jax.experimental import pallas as pl
from jax.experimental.pallas import tpu as pltpu

F32 = jnp.float32
BF16 = jnp.bfloat16

EPS = 1e-6
CHUNK = 64
HEADS = 8
HEAD_DIM = 128
DN_CONV = 4
CM_KERNEL = 31
LANES = 128
SUBLANES = 8
VMEM_LIMIT_BYTES = 56 * 1024 * 1024
CHUNKS_PER_ITER = 4


def _mm(a, b):
    return jnp.dot(a.astype(BF16), b.astype(BF16), preferred_element_type=F32)


def _mm_nt(a, b):
    return lax.dot_general(a.astype(BF16), b.astype(BF16), (((1,), (1,)), ((), ())),
                           preferred_element_type=F32)


def _mm_tn(a, b):
    return lax.dot_general(a.astype(BF16), b.astype(BF16), (((0,), (0,)), ((), ())),
                           preferred_element_type=F32)


def _split(a):
    hi = a.astype(BF16)
    lo = (a - hi.astype(F32)).astype(BF16)
    return hi, lo


def _sigmoid(x):
    return 1.0 / (1.0 + jnp.exp(-x))


def _silu(x):
    return x * _sigmoid(x)


def _inproj_kernel(x_ref, nw_ref, w_ref, wba_ref, p_ref, ba_ref, h_ref):
    j = pl.program_id(1)

    @pl.when(j == 0)
    def _():
        x = x_ref[...]
        ms = jnp.mean(x * x, axis=-1, keepdims=True)
        h = x * lax.rsqrt(ms + EPS) * nw_ref[...]
        hh, hl = _split(h)
        h_ref[...] = hh
        r = jnp.dot(hh, wba_ref[...], preferred_element_type=F32)
        r2 = jnp.dot(hl, wba_ref[:, :LANES], preferred_element_type=F32)
        ba_ref[...] = r[:, :LANES] + (r[:, LANES:] + r2)

    p_ref[...] = jnp.dot(h_ref[...], w_ref[...], preferred_element_type=F32).astype(p_ref.dtype)


def _inproj(x2, norm_w, w_main, w_ba, *, tm, tn):
    n, d = x2.shape
    cols = w_main.shape[1]
    return pl.pallas_call(
        _inproj_kernel,
        grid=(n // tm, cols // tn),
        in_specs=[
            pl.BlockSpec((tm, d), lambda i, j: (i, 0)),
            pl.BlockSpec((1, d), lambda i, j: (0, 0)),
            pl.BlockSpec((d, tn), lambda i, j: (0, j)),
            pl.BlockSpec((d, 2 * LANES), lambda i, j: (0, 0)),
        ],
        out_specs=[
            pl.BlockSpec((tm, tn), lambda i, j: (i, j)),
            pl.BlockSpec((tm, LANES), lambda i, j: (i, 0)),
        ],
        out_shape=[
            jax.ShapeDtypeStruct((n, cols), BF16),
            jax.ShapeDtypeStruct((n, LANES), F32),
        ],
        scratch_shapes=[pltpu.VMEM((tm, d), BF16)],
        compiler_params=pltpu.CompilerParams(
            dimension_semantics=("arbitrary", "arbitrary"),
            vmem_limit_bytes=VMEM_LIMIT_BYTES),
        name="inproj",
    )(x2, norm_w, w_main, w_ba)


def _delta_kernel(qkv_ref, z_ref, ba_ref, cw_ref, alog_ref, dtb_ref, nw_ref, wo_ref, out_ref,
                  xh_ref, act_ref, g_ref, gt_ref, s_ref, o_ref, *, tt):
    width = HEADS * HEAD_DIM
    nchunk = tt // CHUNK
    nslab = 3 * HEADS
    hist = SUBLANES
    t = pl.program_id(1)

    @pl.when(t == 0)
    def _():
        xh_ref[:, 0:hist, :] = jnp.zeros((nslab, hist, HEAD_DIM), F32)
        s_ref[...] = jnp.zeros_like(s_ref)

    for j in range(nslab):
        xh_ref[j, hist:hist + tt, :] = qkv_ref[:, j * HEAD_DIM:(j + 1) * HEAD_DIM].astype(F32)

    rb = 4 * SUBLANES

    def conv_body(j, carry):
        is_qk = j < 2 * HEADS
        post = jnp.where(j < HEADS, HEAD_DIM ** -0.5, 1.0).astype(F32)
        for r0 in range(0, tt, rb):
            acc = None
            for k in range(DN_CONV):
                lo = r0 + hist - (DN_CONV - 1) + k
                term = xh_ref[j, lo:lo + rb, :] * cw_ref[j, k:k + 1, :]
                acc = term if acc is None else acc + term
            y = _silu(acc)
            inv = lax.rsqrt(jnp.sum(y * y, axis=-1, keepdims=True) + EPS)
            act_ref[j, r0:r0 + rb, :] = y * (jnp.where(is_qk, inv, 1.0) * post)
        return carry

    lax.fori_loop(0, nslab, conv_body, 0, unroll=3)
    xh_ref[:, 0:hist, :] = xh_ref[:, tt:tt + hist, :]

    ba = ba_ref[...]
    lane = lax.broadcasted_iota(jnp.int32, (tt, LANES), 1)
    beta = _sigmoid(ba)
    sp_in = ba + dtb_ref[...]
    softplus = jnp.maximum(sp_in, 0.0) + jnp.log1p(jnp.exp(-jnp.abs(sp_in)))
    g = -jnp.exp(alog_ref[...]) * softplus
    g = jnp.where((lane >= HEADS) & (lane < 2 * HEADS), g, 0.0)
    row = lax.broadcasted_iota(jnp.int32, (tt, tt), 0)
    col = lax.broadcasted_iota(jnp.int32, (tt, tt), 1)
    tri = jnp.where((row // CHUNK == col // CHUNK) & (row >= col), 1.0, 0.0).astype(BF16)
    gh, gl = _split(g)
    gc = jnp.dot(tri, gh, preferred_element_type=F32) + jnp.dot(tri, gl, preferred_element_type=F32)
    gfull = jnp.where(lane < HEADS, beta, gc)
    g_ref[...] = gfull
    gt = gfull.T
    for c in range(nchunk):
        gt_ref[c] = gt[:, c * CHUNK:(c + 1) * CHUNK]

    ri = lax.broadcasted_iota(jnp.int32, (CHUNK, CHUNK), 0)
    ci = lax.broadcasted_iota(jnp.int32, (CHUNK, CHUNK), 1)
    tril = ri >= ci
    eye = jnp.where(ri == ci, 1.0, 0.0).astype(F32)
    level_masks = []
    s = 1
    while s < CHUNK:
        level_masks.append((ri // (2 * s) == ci // (2 * s)) & ((ri // s) % 2 == 1) & ((ci // s) % 2 == 0))
        s *= 2

    cpi = min(CHUNKS_PER_ITER, nchunk)

    def chunk_body(ci, carry):
        ps = [(cc, h) for cc in range(cpi) for h in range(HEADS)]
        r0 = [pl.multiple_of((ci * cpi + cc) * CHUNK, CHUNK) for cc in range(cpi)]
        gcb = [g_ref[pl.ds(r0[cc], CHUNK), :] for cc in range(cpi)]
        gtb = [gt_ref[ci * cpi + cc] for cc in range(cpi)]
        beta_c = [gcb[cc][:, h:h + 1] for cc, h in ps]
        gc_c = [gcb[cc][:, HEADS + h:HEADS + h + 1] for cc, h in ps]
        gc_r = [gtb[cc][HEADS + h:HEADS + h + 1, :] for cc, h in ps]
        q = [act_ref[h, pl.ds(r0[cc], CHUNK), :] for cc, h in ps]
        k = [act_ref[HEADS + h, pl.ds(r0[cc], CHUNK), :] for cc, h in ps]
        v = [act_ref[2 * HEADS + h, pl.ds(r0[cc], CHUNK), :] for cc, h in ps]
        n = range(len(ps))

        decay = [jnp.exp(jnp.where(tril, gc_c[i] - gc_r[i], -jnp.inf)) for i in n]
        kq = [_mm_nt(jnp.concatenate([k[i], q[i]], axis=0), k[i]) for i in n]
        m = [beta_c[i] * kq[i][:CHUNK] * decay[i] for i in n]
        attn = [kq[i][CHUNK:] * decay[i] for i in n]

        x = [eye - jnp.where(level_masks[0], m[i], 0.0) for i in n]
        for lm in level_masks[1:]:
            pm = [_mm(jnp.where(lm, m[i], 0.0), x[i]) for i in n]
            x = [x[i] - _mm(x[i], pm[i]) for i in n]

        egc = [jnp.exp(gc_c[i]) for i in n]
        rhs = [jnp.concatenate([v[i] * beta_c[i], k[i] * (beta_c[i] * egc[i])], axis=1) for i in n]
        uw = [_mm(x[i], rhs[i]) for i in n]

        g_last = [gc_c[i][CHUNK - 1:CHUNK, :] for i in n]
        k_dec = [k[i] * jnp.exp(g_last[i] - gc_c[i]) for i in n]
        bg = [_mm_tn(k_dec[i], uw[i]) for i in n]
        ao = [_mm(attn[i], uw[i]) for i in n]
        gq = [jnp.concatenate([bg[i][:, HEAD_DIM:], q[i] * egc[i] - ao[i][:, HEAD_DIM:]], axis=0) for i in n]
        st = [s_ref[h] for h in range(HEADS)]
        for cc in range(cpi):
            idx = [cc * HEADS + h for h in range(HEADS)]
            r = [_mm(gq[i], st[h]) for h, i in enumerate(idx)]
            for h, i in enumerate(idx):
                o_ref[pl.ds(r0[cc], CHUNK), h * HEAD_DIM:(h + 1) * HEAD_DIM] = r[h][HEAD_DIM:] + ao[i][:, :HEAD_DIM]
            st = [st[h] * jnp.exp(g_last[i]) - r[h][:HEAD_DIM] + bg[i][:, :HEAD_DIM] for h, i in enumerate(idx)]
        for h in range(HEADS):
            s_ref[h] = st[h]
        return carry

    lax.fori_loop(0, nchunk // cpi, chunk_body, 0)

    parts = []
    for h in range(HEADS):
        o = o_ref[:, h * HEAD_DIM:(h + 1) * HEAD_DIM]
        z = z_ref[:, h * HEAD_DIM:(h + 1) * HEAD_DIM].astype(F32)
        on = o * lax.rsqrt(jnp.mean(o * o, axis=-1, keepdims=True) + EPS) * nw_ref[...]
        parts.append((on * _silu(z)).astype(BF16))
    og = jnp.concatenate(parts, axis=1)
    out_ref[...] = jnp.dot(og, wo_ref[...], preferred_element_type=F32).astype(out_ref.dtype)


def _delta_branch(p, ba, conv_w, alog, dtb, norm_w, w_o, *, batch, seq, tt):
    width = HEADS * HEAD_DIM
    nt = seq // tt
    d_out = w_o.shape[1]
    kern = functools.partial(_delta_kernel, tt=tt)
    row = lambda b, t: b * nt + t
    return pl.pallas_call(
        kern,
        grid=(batch, nt),
        in_specs=[
            pl.BlockSpec((tt, 3 * width), lambda b, t: (row(b, t), 0)),
            pl.BlockSpec((tt, width), lambda b, t: (row(b, t), 3)),
            pl.BlockSpec((tt, LANES), lambda b, t: (row(b, t), 0)),
            pl.BlockSpec((3 * HEADS, SUBLANES, HEAD_DIM), lambda b, t: (0, 0, 0)),
            pl.BlockSpec((1, LANES), lambda b, t: (0, 0)),
            pl.BlockSpec((1, LANES), lambda b, t: (0, 0)),
            pl.BlockSpec((1, HEAD_DIM), lambda b, t: (0, 0)),
            pl.BlockSpec((width, d_out), lambda b, t: (0, 0)),
        ],
        out_specs=pl.BlockSpec((tt, d_out), lambda b, t: (row(b, t), 0)),
        out_shape=jax.ShapeDtypeStruct((batch * seq, d_out), BF16),
        scratch_shapes=[
            pltpu.VMEM((3 * HEADS, SUBLANES + tt, HEAD_DIM), F32),
            pltpu.VMEM((3 * HEADS, tt, HEAD_DIM), F32),
            pltpu.VMEM((tt, LANES), F32),
            pltpu.VMEM((tt // CHUNK, LANES, CHUNK), F32),
            pltpu.VMEM((HEADS, HEAD_DIM, HEAD_DIM), F32),
            pltpu.VMEM((tt, width), F32),
        ],
        compiler_params=pltpu.CompilerParams(
            dimension_semantics=("arbitrary", "arbitrary"),
            vmem_limit_bytes=VMEM_LIMIT_BYTES),
        name="delta_branch",
    )(p, p, ba, conv_w, alog, dtb, norm_w, w_o)


def _convmod_kernel(glu_ref, bglu_ref, dw_ref, lnw_ref, lnb_ref, w2_ref, b2_ref, out_ref,
                    xh_ref, y_ref, c_ref, *, tt):
    width = lnw_ref.shape[1]
    nslab = width // LANES
    hist = 4 * SUBLANES
    t = pl.program_id(1)

    @pl.when(t == 0)
    def _():
        xh_ref[:, 0:hist, :] = jnp.zeros((nslab, hist, LANES), F32)

    for j in range(nslab):
        ca = slice(j * LANES, (j + 1) * LANES)
        cb = slice(width + j * LANES, width + (j + 1) * LANES)
        ga = glu_ref[:, ca].astype(F32) + bglu_ref[:, ca]
        gb = glu_ref[:, cb].astype(F32) + bglu_ref[:, cb]
        xh_ref[j, hist:hist + tt, :] = ga * _sigmoid(gb)

    rb = 4 * SUBLANES

    def conv_body(j, carry):
        for r0 in range(0, tt, rb):
            acc = None
            for k in range(CM_KERNEL):
                lo = r0 + hist - (CM_KERNEL - 1) + k
                term = xh_ref[j, lo:lo + rb, :] * dw_ref[j, k:k + 1, :]
                acc = term if acc is None else acc + term
            y_ref[j, r0:r0 + rb, :] = acc + dw_ref[j, CM_KERNEL:CM_KERNEL + 1, :]
        return carry

    lax.fori_loop(0, nslab, conv_body, 0)
    xh_ref[:, 0:hist, :] = xh_ref[:, tt:tt + hist, :]

    tot = y_ref[0]
    for j in range(1, nslab):
        tot = tot + y_ref[j]
    mu = jnp.sum(tot, axis=-1, keepdims=True) * (1.0 / width)
    sq = None
    for j in range(nslab):
        cc = y_ref[j] - mu
        sq = cc * cc if sq is None else sq + cc * cc
    rstd = lax.rsqrt(jnp.sum(sq, axis=-1, keepdims=True) * (1.0 / width) + EPS)
    for j in range(nslab):
        ca = slice(j * LANES, (j + 1) * LANES)
        y = (y_ref[j] - mu) * rstd * lnw_ref[:, ca] + lnb_ref[:, ca]
        c_ref[:, ca] = _silu(y).astype(BF16)

    out = jnp.dot(c_ref[...], w2_ref[...], preferred_element_type=F32) + b2_ref[...]
    out_ref[...] = out.astype(out_ref.dtype)


def _conv_branch(p, b_glu, dw_slabs, ln_w, ln_b, w_pw2, b_pw2, *, batch, seq, tt):
    width = ln_w.shape[1]
    nslab = width // LANES
    d_out = w_pw2.shape[1]
    nt = seq // tt
    kern = functools.partial(_convmod_kernel, tt=tt)
    row = lambda b, t: b * nt + t
    const = lambda b, t: (0, 0)
    return pl.pallas_call(
        kern,
        grid=(batch, nt),
        in_specs=[
            pl.BlockSpec((tt, 2 * width), lambda b, t: (row(b, t), 2)),
            pl.BlockSpec((1, 2 * width), const),
            pl.BlockSpec((nslab, CM_KERNEL + 1, LANES), lambda b, t: (0, 0, 0)),
            pl.BlockSpec((1, width), const),
            pl.BlockSpec((1, width), const),
            pl.BlockSpec((width, d_out), const),
            pl.BlockSpec((1, d_out), const),
        ],
        out_specs=pl.BlockSpec((tt, d_out), lambda b, t: (row(b, t), 0)),
        out_shape=jax.ShapeDtypeStruct((batch * seq, d_out), BF16),
        scratch_shapes=[
            pltpu.VMEM((nslab, 4 * SUBLANES + tt, LANES), F32),
            pltpu.VMEM((nslab, tt, LANES), F32),
            pltpu.VMEM((tt, width), BF16),
        ],
        compiler_params=pltpu.CompilerParams(
            dimension_semantics=("arbitrary", "arbitrary"),
            vmem_limit_bytes=VMEM_LIMIT_BYTES),
        name="conv_branch",
    )(p, b_glu, dw_slabs, ln_w, ln_b, w_pw2, b_pw2)


def _merge_ffn_kernel(x_ref, oa_ref, ob_ref, gates_ref, bg_ref, wout_ref, n2_ref, wgu_ref, wdn_ref,
                      nf_ref, out_ref):
    d = x_ref.shape[1]
    hidden = wdn_ref.shape[0]
    gate_a = _sigmoid(gates_ref[:, :d].astype(F32) + bg_ref[:, :d])
    gate_b = _sigmoid(gates_ref[:, d:].astype(F32) + bg_ref[:, d:])
    merged = gate_a * oa_ref[...].astype(F32) + gate_b * ob_ref[...].astype(F32)
    x1 = x_ref[...] + jnp.dot(merged.astype(BF16), wout_ref[...], preferred_element_type=F32)
    h2 = x1 * lax.rsqrt(jnp.mean(x1 * x1, axis=-1, keepdims=True) + EPS) * n2_ref[...]
    gu = jnp.dot(h2.astype(BF16), wgu_ref[...], preferred_element_type=F32)
    act = (_silu(gu[:, :hidden]) * gu[:, hidden:]).astype(BF16)
    x2 = x1 + jnp.dot(act, wdn_ref[...], preferred_element_type=F32)
    out_ref[...] = x2 * lax.rsqrt(jnp.mean(x2 * x2, axis=-1, keepdims=True) + EPS) * nf_ref[...]


def _merge_ffn(x2d, out_a, out_b, p, b_gates, w_out, norm2_w, w_gu, w_dn, norm_f_w, *, tm):
    n, d = x2d.shape
    hidden = w_dn.shape[0]
    const = lambda i: (0, 0)
    single = dict(pipeline_mode=pl.Buffered(1))
    return pl.pallas_call(
        _merge_ffn_kernel,
        grid=(n // tm,),
        in_specs=[
            pl.BlockSpec((tm, d), lambda i: (i, 0)),
            pl.BlockSpec((tm, d), lambda i: (i, 0)),
            pl.BlockSpec((tm, d), lambda i: (i, 0)),
            pl.BlockSpec((tm, 2 * d), lambda i: (i, 3)),
            pl.BlockSpec((1, 2 * d), const),
            pl.BlockSpec((d, d), const, **single),
            pl.BlockSpec((1, d), const),
            pl.BlockSpec((d, 2 * hidden), const, **single),
            pl.BlockSpec((hidden, d), const, **single),
            pl.BlockSpec((1, d), const),
        ],
        out_specs=pl.BlockSpec((tm, d), lambda i: (i, 0)),
        out_shape=jax.ShapeDtypeStruct((n, d), F32),
        compiler_params=pltpu.CompilerParams(
            dimension_semantics=("arbitrary",),
            vmem_limit_bytes=VMEM_LIMIT_BYTES),
        name="merge_ffn",
    )(x2d, out_a, out_b, p, b_gates, w_out, norm2_w, w_gu, w_dn, norm_f_w)


def _pick_tile(n, target):
    t = min(n, target)
    while n % t:
        t //= 2
    return t


def kernel(x, norm1_w, w_in, b_glu, b_gates, dn_conv_w, dn_A_log, dn_dt_bias, dn_norm_w, dn_w_o,
           cm_dw_w, cm_dw_b, cm_ln_w, cm_ln_b, cm_w_pw2, cm_b_pw2, w_out, norm2_w, ffn_w_gate_up,
           ffn_w_down, norm_f_w):
    batch, seq, d = x.shape
    depth = w_in.shape[0]
    width = HEADS * HEAD_DIM
    assert d == width and cm_ln_w.shape[1] == d and seq % CHUNK == 0
    n = batch * seq
    tt = _pick_tile(seq, 256)
    xf = x.reshape(n, d)

    for l in range(depth):
        wl = w_in[l]
        w_main = jnp.concatenate([wl[:, :4 * width], wl[:, 4 * width + 2 * HEADS:]], axis=1).astype(BF16)
        w_ba = jnp.pad(wl[:, 4 * width:4 * width + 2 * HEADS], ((0, 0), (0, LANES - 2 * HEADS)))
        w_ba_hi = w_ba.astype(BF16)
        w_ba_lo = (w_ba - w_ba_hi.astype(F32)).astype(BF16)
        w_ba2 = jnp.concatenate([w_ba_hi, w_ba_lo], axis=1)
        lane_pad = lambda v: jnp.pad(v.reshape(1, HEADS), ((0, 0), (HEADS, LANES - 2 * HEADS)))

        p, ba = _inproj(xf, norm1_w[l].reshape(1, d), w_main, w_ba2,
                        tm=_pick_tile(n, 1024), tn=1024)
        dn_cw = jnp.pad(dn_conv_w[l].reshape(DN_CONV, 3 * HEADS, HEAD_DIM).transpose(1, 0, 2),
                        ((0, 0), (0, SUBLANES - DN_CONV), (0, 0)))
        cm_cw = jnp.concatenate([cm_dw_w[l], cm_dw_b[l][None, :]], axis=0)
        cm_cw = cm_cw.reshape(CM_KERNEL + 1, d // LANES, LANES).transpose(1, 0, 2)

        out_a = _delta_branch(p, ba, dn_cw, lane_pad(dn_A_log[l]), lane_pad(dn_dt_bias[l]),
                              dn_norm_w[l].reshape(1, HEAD_DIM), dn_w_o[l].astype(BF16),
                              batch=batch, seq=seq, tt=tt)
        out_b = _conv_branch(p, b_glu[l].reshape(1, 2 * d), cm_cw,
                             cm_ln_w[l].reshape(1, d), cm_ln_b[l].reshape(1, d),
                             cm_w_pw2[l].astype(BF16), cm_b_pw2[l].reshape(1, d),
                             batch=batch, seq=seq, tt=tt)
        assert depth == 1
        xf = _merge_ffn(xf, out_a, out_b, p, b_gates[l].reshape(1, 2 * d), w_out[l].astype(BF16),
                        norm2_w[l].reshape(1, d), ffn_w_gate_up[l].astype(BF16),
                        ffn_w_down[l].astype(BF16), norm_f_w.reshape(1, d), tm=_pick_tile(n, 256))
    return xf.reshape(batch, seq, d)
```

```python
import functools

import jax
import jax.numpy as jnp
from jax import lax
from jax.experimental import pallas as pl
from jax.experimental.pallas import tpu as pltpu

F32 = jnp.float32
BF16 = jnp.bfloat16

EPS = 1e-6
CHUNK = 64
HEADS = 8
HEAD_DIM = 128
DN_CONV = 4
CM_KERNEL = 31
LANES = 128
SUBLANES = 8
VMEM_LIMIT_BYTES = 56 * 1024 * 1024
CHUNKS_PER_ITER = 4


def _mm(a, b):
    return jnp.dot(a.astype(BF16), b.astype(BF16), preferred_element_type=F32)


def _mm_nt(a, b):
    return lax.dot_general(a.astype(BF16), b.astype(BF16), (((1,), (1,)), ((), ())),
                           preferred_element_type=F32)


def _mm_tn(a, b):
    return lax.dot_general(a.astype(BF16), b.astype(BF16), (((0,), (0,)), ((), ())),
                           preferred_element_type=F32)


def _split(a):
    hi = a.astype(BF16)
    lo = (a - hi.astype(F32)).astype(BF16)
    return hi, lo


def _sigmoid(x):
    return 1.0 / (1.0 + jnp.exp(-x))


def _silu(x):
    return x * _sigmoid(x)


def _inproj_kernel(x_ref, nw_ref, w_ref, wba_ref, p_ref, ba_ref, h_ref):
    j = pl.program_id(1)

    @pl.when(j == 0)
    def _():
        x = x_ref[...]
        ms = jnp.mean(x * x, axis=-1, keepdims=True)
        h = x * lax.rsqrt(ms + EPS) * nw_ref[...]
        hh, hl = _split(h)
        h_ref[...] = hh
        r = jnp.dot(hh, wba_ref[...], preferred_element_type=F32)
        r2 = jnp.dot(hl, wba_ref[:, :LANES], preferred_element_type=F32)
        ba_ref[...] = r[:, :LANES] + (r[:, LANES:] + r2)

    p_ref[...] = jnp.dot(h_ref[...], w_ref[...], preferred_element_type=F32).astype(p_ref.dtype)


def _inproj(x2, norm_w, w_main, w_ba, *, tm, tn):
    n, d = x2.shape
    cols = w_main.shape[1]
    return pl.pallas_call(
        _inproj_kernel,
        grid=(n // tm, cols // tn),
        in_specs=[
            pl.BlockSpec((tm, d), lambda i, j: (i, 0)),
            pl.BlockSpec((1, d), lambda i, j: (0, 0)),
            pl.BlockSpec((d, tn), lambda i, j: (0, j),
                         **(dict(pipeline_mode=pl.Buffered(1)) if tn == cols else {})),
            pl.BlockSpec((d, 2 * LANES), lambda i, j: (0, 0)),
        ],
        out_specs=[
            pl.BlockSpec((tm, tn), lambda i, j: (i, j)),
            pl.BlockSpec((tm, LANES), lambda i, j: (i, 0)),
        ],
        out_shape=[
            jax.ShapeDtypeStruct((n, cols), BF16),
            jax.ShapeDtypeStruct((n, LANES), F32),
        ],
        scratch_shapes=[pltpu.VMEM((tm, d), BF16)],
        compiler_params=pltpu.CompilerParams(
            dimension_semantics=("arbitrary", "arbitrary"),
            vmem_limit_bytes=VMEM_LIMIT_BYTES),
        name="inproj",
    )(x2, norm_w, w_main, w_ba)


def _delta_kernel(qkv_ref, z_ref, ba_ref, cw_ref, alog_ref, dtb_ref, nw_ref, wo_ref, out_ref,
                  xh_ref, act_ref, g_ref, gt_ref, s_ref, o_ref, *, tt):
    width = HEADS * HEAD_DIM
    nchunk = tt // CHUNK
    nslab = 3 * HEADS
    hist = SUBLANES
    t = pl.program_id(1)

    @pl.when(t == 0)
    def _():
        xh_ref[:, 0:hist, :] = jnp.zeros((nslab, hist, HEAD_DIM), F32)
        s_ref[...] = jnp.zeros_like(s_ref)

    for j in range(nslab):
        xh_ref[j, hist:hist + tt, :] = qkv_ref[:, j * HEAD_DIM:(j + 1) * HEAD_DIM].astype(F32)

    rb = 4 * SUBLANES

    def conv_body(j, carry):
        is_qk = j < 2 * HEADS
        post = jnp.where(j < HEADS, HEAD_DIM ** -0.5, 1.0).astype(F32)
        for r0 in range(0, tt, rb):
            acc = None
            for k in range(DN_CONV):
                lo = r0 + hist - (DN_CONV - 1) + k
                term = xh_ref[j, lo:lo + rb, :] * cw_ref[j, k:k + 1, :]
                acc = term if acc is None else acc + term
            y = _silu(acc)
            inv = lax.rsqrt(jnp.sum(y * y, axis=-1, keepdims=True) + EPS)
            act_ref[j, r0:r0 + rb, :] = y * (jnp.where(is_qk, inv, 1.0) * post)
        return carry

    lax.fori_loop(0, nslab, conv_body, 0, unroll=3)
    xh_ref[:, 0:hist, :] = xh_ref[:, tt:tt + hist, :]

    ba = ba_ref[...]
    lane = lax.broadcasted_iota(jnp.int32, (tt, LANES), 1)
    beta = _sigmoid(ba)
    sp_in = ba + dtb_ref[...]
    softplus = jnp.maximum(sp_in, 0.0) + jnp.log1p(jnp.exp(-jnp.abs(sp_in)))
    g = -jnp.exp(alog_ref[...]) * softplus
    g = jnp.where((lane >= HEADS) & (lane < 2 * HEADS), g, 0.0)
    row = lax.broadcasted_iota(jnp.int32, (tt, tt), 0)
    col = lax.broadcasted_iota(jnp.int32, (tt, tt), 1)
    tri = jnp.where((row // CHUNK == col // CHUNK) & (row >= col), 1.0, 0.0).astype(BF16)
    gh, gl = _split(g)
    gc = jnp.dot(tri, gh, preferred_element_type=F32) + jnp.dot(tri, gl, preferred_element_type=F32)
    gfull = jnp.where(lane < HEADS, beta, gc)
    g_ref[...] = gfull
    gt = gfull.T
    for c in range(nchunk):
        gt_ref[c] = gt[:, c * CHUNK:(c + 1) * CHUNK]

    ri = lax.broadcasted_iota(jnp.int32, (CHUNK, CHUNK), 0)
    ci = lax.broadcasted_iota(jnp.int32, (CHUNK, CHUNK), 1)
    tril = ri >= ci
    eye = jnp.where(ri == ci, 1.0, 0.0).astype(F32)
    level_masks = []
    s = 1
    while s < CHUNK:
        level_masks.append((ri // (2 * s) == ci // (2 * s)) & ((ri // s) % 2 == 1) & ((ci // s) % 2 == 0))
        s *= 2

    cpi = min(CHUNKS_PER_ITER, nchunk)

    def chunk_body(ci, carry):
        ps = [(cc, h) for cc in range(cpi) for h in range(HEADS)]
        r0 = [pl.multiple_of((ci * cpi + cc) * CHUNK, CHUNK) for cc in range(cpi)]
        gcb = [g_ref[pl.ds(r0[cc], CHUNK), :] for cc in range(cpi)]
        gtb = [gt_ref[ci * cpi + cc] for cc in range(cpi)]
        beta_c = [gcb[cc][:, h:h + 1] for cc, h in ps]
        gc_c = [gcb[cc][:, HEADS + h:HEADS + h + 1] for cc, h in ps]
        gc_r = [gtb[cc][HEADS + h:HEADS + h + 1, :] for cc, h in ps]
        q = [act_ref[h, pl.ds(r0[cc], CHUNK), :] for cc, h in ps]
        k = [act_ref[HEADS + h, pl.ds(r0[cc], CHUNK), :] for cc, h in ps]
        v = [act_ref[2 * HEADS + h, pl.ds(r0[cc], CHUNK), :] for cc, h in ps]
        n = range(len(ps))

        decay = [jnp.exp(jnp.where(tril, gc_c[i] - gc_r[i], -jnp.inf)) for i in n]
        kq = [_mm_nt(jnp.concatenate([k[i], q[i]], axis=0), k[i]) for i in n]
        m = [beta_c[i] * kq[i][:CHUNK] * decay[i] for i in n]
        attn = [kq[i][CHUNK:] * decay[i] for i in n]

        x = [eye - jnp.where(level_masks[0], m[i], 0.0) for i in n]
        for lm in level_masks[1:]:
            pm = [_mm(jnp.where(lm, m[i], 0.0), x[i]) for i in n]
            x = [x[i] - _mm(x[i], pm[i]) for i in n]

        egc = [jnp.exp(gc_c[i]) for i in n]
        rhs = [jnp.concatenate([v[i] * beta_c[i], k[i] * (beta_c[i] * egc[i])], axis=1) for i in n]
        uw = [_mm(x[i], rhs[i]) for i in n]

        g_last = [gc_c[i][CHUNK - 1:CHUNK, :] for i in n]
        k_dec = [k[i] * jnp.exp(g_last[i] - gc_c[i]) for i in n]
        bg = [_mm_tn(k_dec[i], uw[i]) for i in n]
        ao = [_mm(attn[i], uw[i]) for i in n]
        gq = [jnp.concatenate([bg[i][:, HEAD_DIM:], q[i] * egc[i] - ao[i][:, HEAD_DIM:]], axis=0) for i in n]
        st = [s_ref[h] for h in range(HEADS)]
        for cc in range(cpi):
            idx = [cc * HEADS + h for h in range(HEADS)]
            r = [_mm(gq[i], st[h]) for h, i in enumerate(idx)]
            for h, i in enumerate(idx):
                o_ref[pl.ds(r0[cc], CHUNK), h * HEAD_DIM:(h + 1) * HEAD_DIM] = r[h][HEAD_DIM:] + ao[i][:, :HEAD_DIM]
            st = [st[h] * jnp.exp(g_last[i]) - r[h][:HEAD_DIM] + bg[i][:, :HEAD_DIM] for h, i in enumerate(idx)]
        for h in range(HEADS):
            s_ref[h] = st[h]
        return carry

    lax.fori_loop(0, nchunk // cpi, chunk_body, 0)

    parts = []
    for h in range(HEADS):
        o = o_ref[:, h * HEAD_DIM:(h + 1) * HEAD_DIM]
        z = z_ref[:, h * HEAD_DIM:(h + 1) * HEAD_DIM].astype(F32)
        on = o * lax.rsqrt(jnp.mean(o * o, axis=-1, keepdims=True) + EPS) * nw_ref[...]
        parts.append((on * _silu(z)).astype(BF16))
    og = jnp.concatenate(parts, axis=1)
    out_ref[...] = jnp.dot(og, wo_ref[...], preferred_element_type=F32).astype(out_ref.dtype)


def _delta_branch(p, ba, conv_w, alog, dtb, norm_w, w_o, *, batch, seq, tt):
    width = HEADS * HEAD_DIM
    nt = seq // tt
    d_out = w_o.shape[1]
    kern = functools.partial(_delta_kernel, tt=tt)
    row = lambda b, t: b * nt + t
    return pl.pallas_call(
        kern,
        grid=(batch, nt),
        in_specs=[
            pl.BlockSpec((tt, 3 * width), lambda b, t: (row(b, t), 0)),
            pl.BlockSpec((tt, width), lambda b, t: (row(b, t), 3)),
            pl.BlockSpec((tt, LANES), lambda b, t: (row(b, t), 0)),
            pl.BlockSpec((3 * HEADS, SUBLANES, HEAD_DIM), lambda b, t: (0, 0, 0)),
            pl.BlockSpec((1, LANES), lambda b, t: (0, 0)),
            pl.BlockSpec((1, LANES), lambda b, t: (0, 0)),
            pl.BlockSpec((1, HEAD_DIM), lambda b, t: (0, 0)),
            pl.BlockSpec((width, d_out), lambda b, t: (0, 0)),
        ],
        out_specs=pl.BlockSpec((tt, d_out), lambda b, t: (row(b, t), 0)),
        out_shape=jax.ShapeDtypeStruct((batch * seq, d_out), BF16),
        scratch_shapes=[
            pltpu.VMEM((3 * HEADS, SUBLANES + tt, HEAD_DIM), F32),
            pltpu.VMEM((3 * HEADS, tt, HEAD_DIM), F32),
            pltpu.VMEM((tt, LANES), F32),
            pltpu.VMEM((tt // CHUNK, LANES, CHUNK), F32),
            pltpu.VMEM((HEADS, HEAD_DIM, HEAD_DIM), F32),
            pltpu.VMEM((tt, width), F32),
        ],
        compiler_params=pltpu.CompilerParams(
            dimension_semantics=("arbitrary", "arbitrary"),
            vmem_limit_bytes=VMEM_LIMIT_BYTES),
        name="delta_branch",
    )(p, p, ba, conv_w, alog, dtb, norm_w, w_o)


def _convmod_kernel(glu_ref, bglu_ref, dw_ref, lnw_ref, lnb_ref, w2_ref, b2_ref, out_ref,
                    xh_ref, y_ref, c_ref, *, tt):
    width = lnw_ref.shape[1]
    nslab = width // LANES
    hist = 4 * SUBLANES
    t = pl.program_id(1)

    @pl.when(t == 0)
    def _():
        xh_ref[:, 0:hist, :] = jnp.zeros((nslab, hist, LANES), F32)

    for j in range(nslab):
        ca = slice(j * LANES, (j + 1) * LANES)
        cb = slice(width + j * LANES, width + (j + 1) * LANES)
        ga = glu_ref[:, ca].astype(F32) + bglu_ref[:, ca]
        gb = glu_ref[:, cb].astype(F32) + bglu_ref[:, cb]
        xh_ref[j, hist:hist + tt, :] = ga * _sigmoid(gb)

    rb = 4 * SUBLANES

    def conv_body(j, carry):
        for r0 in range(0, tt, rb):
            acc = None
            for k in range(CM_KERNEL):
                lo = r0 + hist - (CM_KERNEL - 1) + k
                term = xh_ref[j, lo:lo + rb, :] * dw_ref[j, k:k + 1, :]
                acc = term if acc is None else acc + term
            y_ref[j, r0:r0 + rb, :] = acc + dw_ref[j, CM_KERNEL:CM_KERNEL + 1, :]
        return carry

    lax.fori_loop(0, nslab, conv_body, 0)
    xh_ref[:, 0:hist, :] = xh_ref[:, tt:tt + hist, :]

    tot = y_ref[0]
    for j in range(1, nslab):
        tot = tot + y_ref[j]
    mu = jnp.sum(tot, axis=-1, keepdims=True) * (1.0 / width)
    sq = None
    for j in range(nslab):
        cc = y_ref[j] - mu
        sq = cc * cc if sq is None else sq + cc * cc
    rstd = lax.rsqrt(jnp.sum(sq, axis=-1, keepdims=True) * (1.0 / width) + EPS)
    for j in range(nslab):
        ca = slice(j * LANES, (j + 1) * LANES)
        y = (y_ref[j] - mu) * rstd * lnw_ref[:, ca] + lnb_ref[:, ca]
        c_ref[:, ca] = _silu(y).astype(BF16)

    out = jnp.dot(c_ref[...], w2_ref[...], preferred_element_type=F32) + b2_ref[...]
    out_ref[...] = out.astype(out_ref.dtype)


def _conv_branch(p, b_glu, dw_slabs, ln_w, ln_b, w_pw2, b_pw2, *, batch, seq, tt):
    width = ln_w.shape[1]
    nslab = width // LANES
    d_out = w_pw2.shape[1]
    nt = seq // tt
    kern = functools.partial(_convmod_kernel, tt=tt)
    row = lambda b, t: b * nt + t
    const = lambda b, t: (0, 0)
    return pl.pallas_call(
        kern,
        grid=(batch, nt),
        in_specs=[
            pl.BlockSpec((tt, 2 * width), lambda b, t: (row(b, t), 2)),
            pl.BlockSpec((1, 2 * width), const),
            pl.BlockSpec((nslab, CM_KERNEL + 1, LANES), lambda b, t: (0, 0, 0)),
            pl.BlockSpec((1, width), const),
            pl.BlockSpec((1, width), const),
            pl.BlockSpec((width, d_out), const),
            pl.BlockSpec((1, d_out), const),
        ],
        out_specs=pl.BlockSpec((tt, d_out), lambda b, t: (row(b, t), 0)),
        out_shape=jax.ShapeDtypeStruct((batch * seq, d_out), BF16),
        scratch_shapes=[
            pltpu.VMEM((nslab, 4 * SUBLANES + tt, LANES), F32),
            pltpu.VMEM((nslab, tt, LANES), F32),
            pltpu.VMEM((tt, width), BF16),
        ],
        compiler_params=pltpu.CompilerParams(
            dimension_semantics=("arbitrary", "arbitrary"),
            vmem_limit_bytes=VMEM_LIMIT_BYTES),
        name="conv_branch",
    )(p, b_glu, dw_slabs, ln_w, ln_b, w_pw2, b_pw2)


def _merge_ffn_kernel(x_ref, oa_ref, ob_ref, gates_ref, bg_ref, wout_ref, n2_ref, wgu_ref, wdn_ref,
                      nf_ref, out_ref):
    d = x_ref.shape[1]
    hidden = wdn_ref.shape[0]
    gate_a = _sigmoid(gates_ref[:, :d].astype(F32) + bg_ref[:, :d])
    gate_b = _sigmoid(gates_ref[:, d:].astype(F32) + bg_ref[:, d:])
    merged = gate_a * oa_ref[...].astype(F32) + gate_b * ob_ref[...].astype(F32)
    x1 = x_ref[...] + jnp.dot(merged.astype(BF16), wout_ref[...], preferred_element_type=F32)
    h2 = x1 * lax.rsqrt(jnp.mean(x1 * x1, axis=-1, keepdims=True) + EPS) * n2_ref[...]
    gu = jnp.dot(h2.astype(BF16), wgu_ref[...], preferred_element_type=F32)
    act = (_silu(gu[:, :hidden]) * gu[:, hidden:]).astype(BF16)
    x2 = x1 + jnp.dot(act, wdn_ref[...], preferred_element_type=F32)
    out_ref[...] = x2 * lax.rsqrt(jnp.mean(x2 * x2, axis=-1, keepdims=True) + EPS) * nf_ref[...]


def _merge_ffn(x2d, out_a, out_b, p, b_gates, w_out, norm2_w, w_gu, w_dn, norm_f_w, *, tm):
    n, d = x2d.shape
    hidden = w_dn.shape[0]
    const = lambda i: (0, 0)
    single = dict(pipeline_mode=pl.Buffered(1))
    return pl.pallas_call(
        _merge_ffn_kernel,
        grid=(n // tm,),
        in_specs=[
            pl.BlockSpec((tm, d), lambda i: (i, 0)),
            pl.BlockSpec((tm, d), lambda i: (i, 0)),
            pl.BlockSpec((tm, d), lambda i: (i, 0)),
            pl.BlockSpec((tm, 2 * d), lambda i: (i, 3)),
            pl.BlockSpec((1, 2 * d), const),
            pl.BlockSpec((d, d), const, **single),
            pl.BlockSpec((1, d), const),
            pl.BlockSpec((d, 2 * hidden), const, **single),
            pl.BlockSpec((hidden, d), const, **single),
            pl.BlockSpec((1, d), const),
        ],
        out_specs=pl.BlockSpec((tm, d), lambda i: (i, 0)),
        out_shape=jax.ShapeDtypeStruct((n, d), F32),
        compiler_params=pltpu.CompilerParams(
            dimension_semantics=("arbitrary",),
            vmem_limit_bytes=VMEM_LIMIT_BYTES),
        name="merge_ffn",
    )(x2d, out_a, out_b, p, b_gates, w_out, norm2_w, w_gu, w_dn, norm_f_w)


def _pick_tile(n, target):
    t = min(n, target)
    while n % t:
        t //= 2
    return t


def kernel(x, norm1_w, w_in, b_glu, b_gates, dn_conv_w, dn_A_log, dn_dt_bias, dn_norm_w, dn_w_o,
           cm_dw_w, cm_dw_b, cm_ln_w, cm_ln_b, cm_w_pw2, cm_b_pw2, w_out, norm2_w, ffn_w_gate_up,
           ffn_w_down, norm_f_w):
    batch, seq, d = x.shape
    depth = w_in.shape[0]
    width = HEADS * HEAD_DIM
    assert d == width and cm_ln_w.shape[1] == d and seq % CHUNK == 0
    n = batch * seq
    tt = _pick_tile(seq, 256)
    xf = x.reshape(n, d)

    for l in range(depth):
        wl = w_in[l]
        w_main = jnp.concatenate([wl[:, :4 * width], wl[:, 4 * width + 2 * HEADS:]], axis=1).astype(BF16)
        w_ba = jnp.pad(wl[:, 4 * width:4 * width + 2 * HEADS], ((0, 0), (0, LANES - 2 * HEADS)))
        w_ba_hi = w_ba.astype(BF16)
        w_ba_lo = (w_ba - w_ba_hi.astype(F32)).astype(BF16)
        w_ba2 = jnp.concatenate([w_ba_hi, w_ba_lo], axis=1)
        lane_pad = lambda v: jnp.pad(v.reshape(1, HEADS), ((0, 0), (HEADS, LANES - 2 * HEADS)))

        p, ba = _inproj(xf, norm1_w[l].reshape(1, d), w_main, w_ba2,
                        tm=_pick_tile(n, 512), tn=w_main.shape[1])
        dn_cw = jnp.pad(dn_conv_w[l].reshape(DN_CONV, 3 * HEADS, HEAD_DIM).transpose(1, 0, 2),
                        ((0, 0), (0, SUBLANES - DN_CONV), (0, 0)))
        cm_cw = jnp.concatenate([cm_dw_w[l], cm_dw_b[l][None, :]], axis=0)
        cm_cw = cm_cw.reshape(CM_KERNEL + 1, d // LANES, LANES).transpose(1, 0, 2)

        out_a = _delta_branch(p, ba, dn_cw, lane_pad(dn_A_log[l]), lane_pad(dn_dt_bias[l]),
                              dn_norm_w[l].reshape(1, HEAD_DIM), dn_w_o[l].astype(BF16),
                              batch=batch, seq=seq, tt=tt)
        out_b = _conv_branch(p, b_glu[l].reshape(1, 2 * d), cm_cw,
                             cm_ln_w[l].reshape(1, d), cm_ln_b[l].reshape(1, d),
                             cm_w_pw2[l].astype(BF16), cm_b_pw2[l].reshape(1, d),
                             batch=batch, seq=seq, tt=tt)
        assert depth == 1
        xf = _merge_ffn(xf, out_a, out_b, p, b_gates[l].reshape(1, 2 * d), w_out[l].astype(BF16),
                        norm2_w[l].reshape(1, d), ffn_w_gate_up[l].astype(BF16),
                        ffn_w_down[l].astype(BF16), norm_f_w.reshape(1, d), tm=_pick_tile(n, 512))
    return xf.reshape(batch, seq, d)
```

```python
import functools

import jax
import jax.numpy as jnp
from jax import lax
from jax.experimental import pallas as pl
from jax.experimental.pallas import tpu as pltpu

F32 = jnp.float32
BF16 = jnp.bfloat16

EPS = 1e-6
CHUNK = 64
HEADS = 8
HEAD_DIM = 128
DN_CONV = 4
CM_KERNEL = 31
LANES = 128
SUBLANES = 8
VMEM_LIMIT_BYTES = 56 * 1024 * 1024
CHUNKS_PER_ITER = 4


def _mm(a, b):
    return jnp.dot(a.astype(BF16), b.astype(BF16), preferred_element_type=F32)


def _mm_nt(a, b):
    return lax.dot_general(a.astype(BF16), b.astype(BF16), (((1,), (1,)), ((), ())),
                           preferred_element_type=F32)


def _mm_tn(a, b):
    return lax.dot_general(a.astype(BF16), b.astype(BF16), (((0,), (0,)), ((), ())),
                           preferred_element_type=F32)


def _split(a):
    hi = a.astype(BF16)
    lo = (a - hi.astype(F32)).astype(BF16)
    return hi, lo


def _sigmoid(x):
    return 1.0 / (1.0 + jnp.exp(-x))


def _silu(x):
    return x * _sigmoid(x)


def _inproj_kernel(x_ref, nw_ref, w_ref, wba_ref, p_ref, ba_ref, h_ref):
    j = pl.program_id(1)

    @pl.when(j == 0)
    def _():
        x = x_ref[...]
        ms = jnp.mean(x * x, axis=-1, keepdims=True)
        h = x * lax.rsqrt(ms + EPS) * nw_ref[...]
        hh, hl = _split(h)
        h_ref[...] = hh
        r = jnp.dot(hh, wba_ref[...], preferred_element_type=F32)
        r2 = jnp.dot(hl, wba_ref[:, :LANES], preferred_element_type=F32)
        ba_ref[...] = r[:, :LANES] + (r[:, LANES:] + r2)

    p_ref[...] = jnp.dot(h_ref[...], w_ref[...], preferred_element_type=F32).astype(p_ref.dtype)


def _inproj(x2, norm_w, w_main, w_ba, *, tm, tn):
    n, d = x2.shape
    cols = w_main.shape[1]
    return pl.pallas_call(
        _inproj_kernel,
        grid=(n // tm, cols // tn),
        in_specs=[
            pl.BlockSpec((tm, d), lambda i, j: (i, 0)),
            pl.BlockSpec((1, d), lambda i, j: (0, 0)),
            pl.BlockSpec((d, tn), lambda i, j: (0, j),
                         **(dict(pipeline_mode=pl.Buffered(1)) if tn == cols else {})),
            pl.BlockSpec((d, 2 * LANES), lambda i, j: (0, 0)),
        ],
        out_specs=[
            pl.BlockSpec((tm, tn), lambda i, j: (i, j)),
            pl.BlockSpec((tm, LANES), lambda i, j: (i, 0)),
        ],
        out_shape=[
            jax.ShapeDtypeStruct((n, cols), BF16),
            jax.ShapeDtypeStruct((n, LANES), F32),
        ],
        scratch_shapes=[pltpu.VMEM((tm, d), BF16)],
        compiler_params=pltpu.CompilerParams(
            dimension_semantics=("arbitrary", "arbitrary"),
            vmem_limit_bytes=VMEM_LIMIT_BYTES),
        name="inproj",
    )(x2, norm_w, w_main, w_ba)


def _delta_kernel(qkv_ref, z_ref, ba_ref, cw_ref, alog_ref, dtb_ref, nw_ref, wo_ref, out_ref,
                  xh_ref, act_ref, g_ref, gt_ref, s_ref, o_ref, *, tt):
    width = HEADS * HEAD_DIM
    nchunk = tt // CHUNK
    nslab = 3 * HEADS
    hist = SUBLANES
    t = pl.program_id(1)

    @pl.when(t == 0)
    def _():
        xh_ref[:, 0:hist, :] = jnp.zeros((nslab, hist, HEAD_DIM), F32)
        s_ref[...] = jnp.zeros_like(s_ref)

    for j in range(nslab):
        xh_ref[j, hist:hist + tt, :] = qkv_ref[:, j * HEAD_DIM:(j + 1) * HEAD_DIM].astype(F32)

    rb = 4 * SUBLANES

    def conv_body(j, carry):
        is_qk = j < 2 * HEADS
        post = jnp.where(j < HEADS, HEAD_DIM ** -0.5, 1.0).astype(F32)
        for r0 in range(0, tt, rb):
            acc = None
            for k in range(DN_CONV):
                lo = r0 + hist - (DN_CONV - 1) + k
                term = xh_ref[j, lo:lo + rb, :] * cw_ref[j, k:k + 1, :]
                acc = term if acc is None else acc + term
            y = _silu(acc)
            inv = lax.rsqrt(jnp.sum(y * y, axis=-1, keepdims=True) + EPS)
            act_ref[j, r0:r0 + rb, :] = y * (jnp.where(is_qk, inv, 1.0) * post)
        return carry

    lax.fori_loop(0, nslab, conv_body, 0, unroll=3)
    xh_ref[:, 0:hist, :] = xh_ref[:, tt:tt + hist, :]

    ba = ba_ref[...]
    lane = lax.broadcasted_iota(jnp.int32, (tt, LANES), 1)
    beta = _sigmoid(ba)
    sp_in = ba + dtb_ref[...]
    softplus = jnp.maximum(sp_in, 0.0) + jnp.log1p(jnp.exp(-jnp.abs(sp_in)))
    g = -jnp.exp(alog_ref[...]) * softplus
    g = jnp.where((lane >= HEADS) & (lane < 2 * HEADS), g, 0.0)
    row = lax.broadcasted_iota(jnp.int32, (tt, tt), 0)
    col = lax.broadcasted_iota(jnp.int32, (tt, tt), 1)
    tri = jnp.where((row // CHUNK == col // CHUNK) & (row >= col), 1.0, 0.0).astype(BF16)
    gh, gl = _split(g)
    gc = jnp.dot(tri, gh, preferred_element_type=F32) + jnp.dot(tri, gl, preferred_element_type=F32)
    gfull = jnp.where(lane < HEADS, beta, gc)
    g_ref[...] = gfull
    gt = gfull.T
    for c in range(nchunk):
        gt_ref[c] = gt[:, c * CHUNK:(c + 1) * CHUNK]

    ri = lax.broadcasted_iota(jnp.int32, (CHUNK, CHUNK), 0)
    ci = lax.broadcasted_iota(jnp.int32, (CHUNK, CHUNK), 1)
    tril = ri >= ci
    eye = jnp.where(ri == ci, 1.0, 0.0).astype(F32)
    level_masks = []
    s = 1
    while s < CHUNK:
        level_masks.append((ri // (2 * s) == ci // (2 * s)) & ((ri // s) % 2 == 1) & ((ci // s) % 2 == 0))
        s *= 2

    cpi = min(CHUNKS_PER_ITER, nchunk)

    def chunk_body(ci, carry):
        ps = [(cc, h) for cc in range(cpi) for h in range(HEADS)]
        r0 = [pl.multiple_of((ci * cpi + cc) * CHUNK, CHUNK) for cc in range(cpi)]
        gcb = [g_ref[pl.ds(r0[cc], CHUNK), :] for cc in range(cpi)]
        gtb = [gt_ref[ci * cpi + cc] for cc in range(cpi)]
        beta_c = [gcb[cc][:, h:h + 1] for cc, h in ps]
        gc_c = [gcb[cc][:, HEADS + h:HEADS + h + 1] for cc, h in ps]
        gc_r = [gtb[cc][HEADS + h:HEADS + h + 1, :] for cc, h in ps]
        q = [act_ref[h, pl.ds(r0[cc], CHUNK), :] for cc, h in ps]
        k = [act_ref[HEADS + h, pl.ds(r0[cc], CHUNK), :] for cc, h in ps]
        v = [act_ref[2 * HEADS + h, pl.ds(r0[cc], CHUNK), :] for cc, h in ps]
        n = range(len(ps))

        decay = [jnp.exp(jnp.where(tril, gc_c[i] - gc_r[i], -jnp.inf)) for i in n]
        kq = [_mm_nt(jnp.concatenate([k[i], q[i]], axis=0), k[i]) for i in n]
        m = [beta_c[i] * kq[i][:CHUNK] * decay[i] for i in n]
        attn = [kq[i][CHUNK:] * decay[i] for i in n]

        x = [eye - jnp.where(level_masks[0], m[i], 0.0) for i in n]
        for lm in level_masks[1:]:
            pm = [_mm(jnp.where(lm, m[i], 0.0), x[i]) for i in n]
            x = [x[i] - _mm(x[i], pm[i]) for i in n]

        egc = [jnp.exp(gc_c[i]) for i in n]
        rhs = [jnp.concatenate([v[i] * beta_c[i], k[i] * (beta_c[i] * egc[i])], axis=1) for i in n]
        uw = [_mm(x[i], rhs[i]) for i in n]

        g_last = [gc_c[i][CHUNK - 1:CHUNK, :] for i in n]
        k_dec = [k[i] * jnp.exp(g_last[i] - gc_c[i]) for i in n]
        bg = [_mm_tn(k_dec[i], uw[i]) for i in n]
        ao = [_mm(attn[i], uw[i]) for i in n]
        gq = [jnp.concatenate([bg[i][:, HEAD_DIM:], q[i] * egc[i] - ao[i][:, HEAD_DIM:]], axis=0) for i in n]
        st = [s_ref[h] for h in range(HEADS)]
        for cc in range(cpi):
            idx = [cc * HEADS + h for h in range(HEADS)]
            r = [_mm(gq[i], st[h]) for h, i in enumerate(idx)]
            for h, i in enumerate(idx):
                o_ref[pl.ds(r0[cc], CHUNK), h * HEAD_DIM:(h + 1) * HEAD_DIM] = r[h][HEAD_DIM:] + ao[i][:, :HEAD_DIM]
            st = [st[h] * jnp.exp(g_last[i]) - r[h][:HEAD_DIM] + bg[i][:, :HEAD_DIM] for h, i in enumerate(idx)]
        for h in range(HEADS):
            s_ref[h] = st[h]
        return carry

    lax.fori_loop(0, nchunk // cpi, chunk_body, 0)

    parts = []
    for h in range(HEADS):
        o = o_ref[:, h * HEAD_DIM:(h + 1) * HEAD_DIM]
        z = z_ref[:, h * HEAD_DIM:(h + 1) * HEAD_DIM].astype(F32)
        on = o * lax.rsqrt(jnp.mean(o * o, axis=-1, keepdims=True) + EPS) * nw_ref[...]
        parts.append((on * _silu(z)).astype(BF16))
    og = jnp.concatenate(parts, axis=1)
    out_ref[...] = jnp.dot(og, wo_ref[...], preferred_element_type=F32).astype(out_ref.dtype)


def _delta_branch(p, ba, conv_w, alog, dtb, norm_w, w_o, *, batch, seq, tt):
    width = HEADS * HEAD_DIM
    nt = seq // tt
    d_out = w_o.shape[1]
    kern = functools.partial(_delta_kernel, tt=tt)
    row = lambda b, t: b * nt + t
    return pl.pallas_call(
        kern,
        grid=(batch, nt),
        in_specs=[
            pl.BlockSpec((tt, 3 * width), lambda b, t: (row(b, t), 0)),
            pl.BlockSpec((tt, width), lambda b, t: (row(b, t), 3)),
            pl.BlockSpec((tt, LANES), lambda b, t: (row(b, t), 0)),
            pl.BlockSpec((3 * HEADS, SUBLANES, HEAD_DIM), lambda b, t: (0, 0, 0)),
            pl.BlockSpec((1, LANES), lambda b, t: (0, 0)),
            pl.BlockSpec((1, LANES), lambda b, t: (0, 0)),
            pl.BlockSpec((1, HEAD_DIM), lambda b, t: (0, 0)),
            pl.BlockSpec((width, d_out), lambda b, t: (0, 0)),
        ],
        out_specs=pl.BlockSpec((tt, d_out), lambda b, t: (row(b, t), 0)),
        out_shape=jax.ShapeDtypeStruct((batch * seq, d_out), BF16),
        scratch_shapes=[
            pltpu.VMEM((3 * HEADS, SUBLANES + tt, HEAD_DIM), F32),
            pltpu.VMEM((3 * HEADS, tt, HEAD_DIM), F32),
            pltpu.VMEM((tt, LANES), F32),
            pltpu.VMEM((tt // CHUNK, LANES, CHUNK), F32),
            pltpu.VMEM((HEADS, HEAD_DIM, HEAD_DIM), F32),
            pltpu.VMEM((tt, width), F32),
        ],
        compiler_params=pltpu.CompilerParams(
            dimension_semantics=("arbitrary", "arbitrary"),
            vmem_limit_bytes=VMEM_LIMIT_BYTES),
        name="delta_branch",
    )(p, p, ba, conv_w, alog, dtb, norm_w, w_o)


def _convmod_kernel(glu_ref, bglu_ref, dw_ref, dwb_ref, lnw_ref, lnb_ref, w2_ref, b2_ref, out_ref,
                    x0_ref, x1_ref, xf_ref, y_ref, c_ref, *, tt):
    width = lnw_ref.shape[1]
    nslab = width // LANES
    hw = 2 * SUBLANES
    rb = 2 * SUBLANES
    group = 4
    t = pl.program_id(1)

    @pl.when(t == 0)
    def _():
        x0_ref[:, 0:hw, :] = jnp.zeros((nslab, hw, LANES), jnp.uint32)
        x1_ref[:, 0:hw, :] = jnp.zeros((nslab, hw, LANES), jnp.uint32)
        xf_ref[:, 0:SUBLANES, :] = jnp.zeros((nslab, SUBLANES, LANES), F32)

    for j in range(nslab):
        ca = slice(j * LANES, (j + 1) * LANES)
        cb = slice(width + j * LANES, width + (j + 1) * LANES)
        ga = glu_ref[:, ca].astype(F32) + bglu_ref[:, ca]
        gb = glu_ref[:, cb].astype(F32) + bglu_ref[:, cb]
        c = ga * _sigmoid(gb)
        xf_ref[j, SUBLANES:SUBLANES + tt, :] = c
        x0_ref[j, hw:hw + tt // 2, :] = pltpu.bitcast(c.astype(BF16), jnp.uint32)
        shifted = xf_ref[j, SUBLANES - 1:SUBLANES - 1 + tt, :]
        x1_ref[j, hw:hw + tt // 2, :] = pltpu.bitcast(shifted.astype(BF16), jnp.uint32)

    def conv_body(j, carry):
        for g0 in range(0, tt, group * rb):
            acc = [None] * group
            for k in range(CM_KERNEL):
                wk = dw_ref[j, k].astype(F32)
                for b in range(group):
                    off = g0 + b * rb - (CM_KERNEL - 1) + k
                    if off % 2 == 0:
                        words = x0_ref[j, hw + off // 2:hw + off // 2 + rb // 2, :]
                    else:
                        words = x1_ref[j, hw + (off + 1) // 2:hw + (off + 1) // 2 + rb // 2, :]
                    term = pltpu.bitcast(words, BF16).astype(F32) * wk
                    acc[b] = term if acc[b] is None else acc[b] + term
            for b in range(group):
                r0 = g0 + b * rb
                y_ref[j, r0:r0 + rb, :] = acc[b] + dwb_ref[j]
        return carry

    lax.fori_loop(0, nslab, conv_body, 0)
    x0_ref[:, 0:hw, :] = x0_ref[:, tt // 2:tt // 2 + hw, :]
    x1_ref[:, 0:hw, :] = x1_ref[:, tt // 2:tt // 2 + hw, :]
    xf_ref[:, 0:SUBLANES, :] = xf_ref[:, tt:tt + SUBLANES, :]

    tot = y_ref[0]
    for j in range(1, nslab):
        tot = tot + y_ref[j]
    mu = jnp.sum(tot, axis=-1, keepdims=True) * (1.0 / width)
    sq = None
    for j in range(nslab):
        cc = y_ref[j] - mu
        sq = cc * cc if sq is None else sq + cc * cc
    rstd = lax.rsqrt(jnp.sum(sq, axis=-1, keepdims=True) * (1.0 / width) + EPS)
    for j in range(nslab):
        ca = slice(j * LANES, (j + 1) * LANES)
        y = (y_ref[j] - mu) * rstd * lnw_ref[:, ca] + lnb_ref[:, ca]
        c_ref[:, ca] = _silu(y).astype(BF16)

    out = jnp.dot(c_ref[...], w2_ref[...], preferred_element_type=F32) + b2_ref[...]
    out_ref[...] = out.astype(out_ref.dtype)


def _conv_branch(p, b_glu, dw_taps, dw_bias, ln_w, ln_b, w_pw2, b_pw2, *, batch, seq, tt):
    width = ln_w.shape[1]
    nslab = width // LANES
    d_out = w_pw2.shape[1]
    nt = seq // tt
    kern = functools.partial(_convmod_kernel, tt=tt)
    row = lambda b, t: b * nt + t
    const = lambda b, t: (0, 0)
    return pl.pallas_call(
        kern,
        grid=(batch, nt),
        in_specs=[
            pl.BlockSpec((tt, 2 * width), lambda b, t: (row(b, t), 2)),
            pl.BlockSpec((1, 2 * width), const),
            pl.BlockSpec((nslab, CM_KERNEL, 2 * SUBLANES, LANES), lambda b, t: (0, 0, 0, 0)),
            pl.BlockSpec((nslab, 1, LANES), lambda b, t: (0, 0, 0)),
            pl.BlockSpec((1, width), const),
            pl.BlockSpec((1, width), const),
            pl.BlockSpec((width, d_out), const),
            pl.BlockSpec((1, d_out), const),
        ],
        out_specs=pl.BlockSpec((tt, d_out), lambda b, t: (row(b, t), 0)),
        out_shape=jax.ShapeDtypeStruct((batch * seq, d_out), BF16),
        scratch_shapes=[
            pltpu.VMEM((nslab, 2 * SUBLANES + tt // 2, LANES), jnp.uint32),
            pltpu.VMEM((nslab, 2 * SUBLANES + tt // 2, LANES), jnp.uint32),
            pltpu.VMEM((nslab, SUBLANES + tt, LANES), F32),
            pltpu.VMEM((nslab, tt, LANES), F32),
            pltpu.VMEM((tt, width), BF16),
        ],
        compiler_params=pltpu.CompilerParams(
            dimension_semantics=("arbitrary", "arbitrary"),
            vmem_limit_bytes=VMEM_LIMIT_BYTES),
        name="conv_branch",
    )(p, b_glu, dw_taps, dw_bias, ln_w, ln_b, w_pw2, b_pw2)


def _merge_ffn_kernel(x_ref, oa_ref, ob_ref, gates_ref, bg_ref, wout_ref, n2_ref, wgu_ref, wdn_ref,
                      nf_ref, out_ref):
    d = x_ref.shape[1]
    hidden = wdn_ref.shape[0]
    gate_a = _sigmoid(gates_ref[:, :d].astype(F32) + bg_ref[:, :d])
    gate_b = _sigmoid(gates_ref[:, d:].astype(F32) + bg_ref[:, d:])
    merged = gate_a * oa_ref[...].astype(F32) + gate_b * ob_ref[...].astype(F32)
    x1 = x_ref[...] + jnp.dot(merged.astype(BF16), wout_ref[...], preferred_element_type=F32)
    h2 = x1 * lax.rsqrt(jnp.mean(x1 * x1, axis=-1, keepdims=True) + EPS) * n2_ref[...]
    gu = jnp.dot(h2.astype(BF16), wgu_ref[...], preferred_element_type=F32)
    act = (_silu(gu[:, :hidden]) * gu[:, hidden:]).astype(BF16)
    x2 = x1 + jnp.dot(act, wdn_ref[...], preferred_element_type=F32)
    out_ref[...] = x2 * lax.rsqrt(jnp.mean(x2 * x2, axis=-1, keepdims=True) + EPS) * nf_ref[...]


def _merge_ffn(x2d, out_a, out_b, p, b_gates, w_out, norm2_w, w_gu, w_dn, norm_f_w, *, tm):
    n, d = x2d.shape
    hidden = w_dn.shape[0]
    const = lambda i: (0, 0)
    single = dict(pipeline_mode=pl.Buffered(1))
    return pl.pallas_call(
        _merge_ffn_kernel,
        grid=(n // tm,),
        in_specs=[
            pl.BlockSpec((tm, d), lambda i: (i, 0)),
            pl.BlockSpec((tm, d), lambda i: (i, 0)),
            pl.BlockSpec((tm, d), lambda i: (i, 0)),
            pl.BlockSpec((tm, 2 * d), lambda i: (i, 3)),
            pl.BlockSpec((1, 2 * d), const),
            pl.BlockSpec((d, d), const, **single),
            pl.BlockSpec((1, d), const),
            pl.BlockSpec((d, 2 * hidden), const, **single),
            pl.BlockSpec((hidden, d), const, **single),
            pl.BlockSpec((1, d), const),
        ],
        out_specs=pl.BlockSpec((tm, d), lambda i: (i, 0)),
        out_shape=jax.ShapeDtypeStruct((n, d), F32),
        compiler_params=pltpu.CompilerParams(
            dimension_semantics=("arbitrary",),
            vmem_limit_bytes=VMEM_LIMIT_BYTES),
        name="merge_ffn",
    )(x2d, out_a, out_b, p, b_gates, w_out, norm2_w, w_gu, w_dn, norm_f_w)


def _pick_tile(n, target):
    t = min(n, target)
    while n % t:
        t //= 2
    return t


def kernel(x, norm1_w, w_in, b_glu, b_gates, dn_conv_w, dn_A_log, dn_dt_bias, dn_norm_w, dn_w_o,
           cm_dw_w, cm_dw_b, cm_ln_w, cm_ln_b, cm_w_pw2, cm_b_pw2, w_out, norm2_w, ffn_w_gate_up,
           ffn_w_down, norm_f_w):
    batch, seq, d = x.shape
    depth = w_in.shape[0]
    width = HEADS * HEAD_DIM
    assert d == width and cm_ln_w.shape[1] == d and seq % CHUNK == 0
    n = batch * seq
    tt = _pick_tile(seq, 256)
    xf = x.reshape(n, d)

    for l in range(depth):
        wl = w_in[l]
        w_main = jnp.concatenate([wl[:, :4 * width], wl[:, 4 * width + 2 * HEADS:]], axis=1).astype(BF16)
        w_ba = jnp.pad(wl[:, 4 * width:4 * width + 2 * HEADS], ((0, 0), (0, LANES - 2 * HEADS)))
        w_ba_hi = w_ba.astype(BF16)
        w_ba_lo = (w_ba - w_ba_hi.astype(F32)).astype(BF16)
        w_ba2 = jnp.concatenate([w_ba_hi, w_ba_lo], axis=1)
        lane_pad = lambda v: jnp.pad(v.reshape(1, HEADS), ((0, 0), (HEADS, LANES - 2 * HEADS)))

        p, ba = _inproj(xf, norm1_w[l].reshape(1, d), w_main, w_ba2,
                        tm=_pick_tile(n, 512), tn=w_main.shape[1])
        dn_cw = jnp.pad(dn_conv_w[l].reshape(DN_CONV, 3 * HEADS, HEAD_DIM).transpose(1, 0, 2),
                        ((0, 0), (0, SUBLANES - DN_CONV), (0, 0)))
        cm_taps = cm_dw_w[l].reshape(CM_KERNEL, d // LANES, LANES).transpose(1, 0, 2).astype(BF16)
        cm_taps = jnp.broadcast_to(cm_taps[:, :, None, :], (d // LANES, CM_KERNEL, 2 * SUBLANES, LANES))
        cm_bias = cm_dw_b[l].reshape(d // LANES, 1, LANES)

        out_a = _delta_branch(p, ba, dn_cw, lane_pad(dn_A_log[l]), lane_pad(dn_dt_bias[l]),
                              dn_norm_w[l].reshape(1, HEAD_DIM), dn_w_o[l].astype(BF16),
                              batch=batch, seq=seq, tt=tt)
        out_b = _conv_branch(p, b_glu[l].reshape(1, 2 * d), cm_taps, cm_bias,
                             cm_ln_w[l].reshape(1, d), cm_ln_b[l].reshape(1, d),
                             cm_w_pw2[l].astype(BF16), cm_b_pw2[l].reshape(1, d),
                             batch=batch, seq=seq, tt=tt)
        assert depth == 1
        xf = _merge_ffn(xf, out_a, out_b, p, b_gates[l].reshape(1, 2 * d), w_out[l].astype(BF16),
                        norm2_w[l].reshape(1, d), ffn_w_gate_up[l].astype(BF16),
                        ffn_w_down[l].astype(BF16), norm_f_w.reshape(1, d), tm=_pick_tile(n, 512))
    return xf.reshape(batch, seq, d)
```

```python
import functools

import jax
import jax.numpy as jnp
from jax import lax
from jax.experimental import pallas as pl
from jax.experimental.pallas import tpu as pltpu

F32 = jnp.float32
BF16 = jnp.bfloat16

EPS = 1e-6
CHUNK = 64
HEADS = 8
HEAD_DIM = 128
DN_CONV = 4
CM_KERNEL = 31
LANES = 128
SUBLANES = 8
VMEM_LIMIT_BYTES = 56 * 1024 * 1024
CHUNKS_PER_ITER = 4


def _mm(a, b):
    return jnp.dot(a.astype(BF16), b.astype(BF16), preferred_element_type=F32)


def _mm_nt(a, b):
    return lax.dot_general(a.astype(BF16), b.astype(BF16), (((1,), (1,)), ((), ())),
                           preferred_element_type=F32)


def _mm_tn(a, b):
    return lax.dot_general(a.astype(BF16), b.astype(BF16), (((0,), (0,)), ((), ())),
                           preferred_element_type=F32)


def _split(a):
    hi = a.astype(BF16)
    lo = (a - hi.astype(F32)).astype(BF16)
    return hi, lo


def _sigmoid(x):
    return 1.0 / (1.0 + jnp.exp(-x))


def _silu(x):
    return x * _sigmoid(x)


def _inproj_kernel(x_ref, nw_ref, w_ref, wba_ref, p_ref, ba_ref, h_ref):
    j = pl.program_id(1)

    @pl.when(j == 0)
    def _():
        x = x_ref[...]
        ms = jnp.mean(x * x, axis=-1, keepdims=True)
        h = x * lax.rsqrt(ms + EPS) * nw_ref[...]
        hh, hl = _split(h)
        h_ref[...] = hh
        r = jnp.dot(hh, wba_ref[...], preferred_element_type=F32)
        r2 = jnp.dot(hl, wba_ref[:, :LANES], preferred_element_type=F32)
        ba_ref[...] = r[:, :LANES] + (r[:, LANES:] + r2)

    p_ref[...] = jnp.dot(h_ref[...], w_ref[...], preferred_element_type=F32).astype(p_ref.dtype)


def _inproj(x2, norm_w, w_main, w_ba, *, tm, tn):
    n, d = x2.shape
    cols = w_main.shape[1]
    return pl.pallas_call(
        _inproj_kernel,
        grid=(n // tm, cols // tn),
        in_specs=[
            pl.BlockSpec((tm, d), lambda i, j: (i, 0)),
            pl.BlockSpec((1, d), lambda i, j: (0, 0)),
            pl.BlockSpec((d, tn), lambda i, j: (0, j),
                         **(dict(pipeline_mode=pl.Buffered(1)) if tn == cols else {})),
            pl.BlockSpec((d, 2 * LANES), lambda i, j: (0, 0)),
        ],
        out_specs=[
            pl.BlockSpec((tm, tn), lambda i, j: (i, j)),
            pl.BlockSpec((tm, LANES), lambda i, j: (i, 0)),
        ],
        out_shape=[
            jax.ShapeDtypeStruct((n, cols), BF16),
            jax.ShapeDtypeStruct((n, LANES), F32),
        ],
        scratch_shapes=[pltpu.VMEM((tm, d), BF16)],
        compiler_params=pltpu.CompilerParams(
            dimension_semantics=("arbitrary", "arbitrary"),
            vmem_limit_bytes=VMEM_LIMIT_BYTES),
        name="inproj",
    )(x2, norm_w, w_main, w_ba)


def _delta_kernel(qkv_ref, z_ref, ba_ref, cw_ref, alog_ref, dtb_ref, nw_ref, wo_ref, out_ref,
                  xh_ref, act_ref, g_ref, gt_ref, s_ref, o_ref, *, tt):
    width = HEADS * HEAD_DIM
    nchunk = tt // CHUNK
    nslab = 3 * HEADS
    hist = SUBLANES
    t = pl.program_id(1)

    @pl.when(t == 0)
    def _():
        xh_ref[:, 0:hist, :] = jnp.zeros((nslab, hist, HEAD_DIM), F32)
        s_ref[...] = jnp.zeros_like(s_ref)

    for j in range(nslab):
        xh_ref[j, hist:hist + tt, :] = qkv_ref[:, j * HEAD_DIM:(j + 1) * HEAD_DIM].astype(F32)

    rb = 4 * SUBLANES

    def conv_slab(j, post):
        for r0 in range(0, tt, rb):
            acc = None
            for k in range(DN_CONV):
                lo = r0 + hist - (DN_CONV - 1) + k
                term = xh_ref[j, lo:lo + rb, :] * cw_ref[j, k:k + 1, :]
                acc = term if acc is None else acc + term
            y = _silu(acc)
            if post is not None:
                y = y * (lax.rsqrt(jnp.sum(y * y, axis=-1, keepdims=True) + EPS) * post)
            act_ref[j, r0:r0 + rb, :] = y

    def qk_body(j, carry):
        conv_slab(j, jnp.where(j < HEADS, HEAD_DIM ** -0.5, 1.0).astype(F32))
        return carry

    def v_body(j, carry):
        conv_slab(j, None)
        return carry

    lax.fori_loop(0, 2 * HEADS, qk_body, 0, unroll=2)
    lax.fori_loop(2 * HEADS, nslab, v_body, 0, unroll=2)
    xh_ref[:, 0:hist, :] = xh_ref[:, tt:tt + hist, :]

    ba = ba_ref[...]
    lane = lax.broadcasted_iota(jnp.int32, (tt, LANES), 1)
    beta = _sigmoid(ba)
    sp_in = ba + dtb_ref[...]
    softplus = jnp.maximum(sp_in, 0.0) + jnp.log1p(jnp.exp(-jnp.abs(sp_in)))
    g = -jnp.exp(alog_ref[...]) * softplus
    g = jnp.where((lane >= HEADS) & (lane < 2 * HEADS), g, 0.0)
    row = lax.broadcasted_iota(jnp.int32, (tt, tt), 0)
    col = lax.broadcasted_iota(jnp.int32, (tt, tt), 1)
    tri = jnp.where((row // CHUNK == col // CHUNK) & (row >= col), 1.0, 0.0).astype(BF16)
    gh, gl = _split(g)
    gc = jnp.dot(tri, gh, preferred_element_type=F32) + jnp.dot(tri, gl, preferred_element_type=F32)
    gfull = jnp.where(lane < HEADS, beta, gc)
    g_ref[...] = gfull
    gt = gfull.T
    for c in range(nchunk):
        gt_ref[c] = gt[:, c * CHUNK:(c + 1) * CHUNK]

    ri = lax.broadcasted_iota(jnp.int32, (CHUNK, CHUNK), 0)
    ci = lax.broadcasted_iota(jnp.int32, (CHUNK, CHUNK), 1)
    tril = ri >= ci
    eye = jnp.where(ri == ci, 1.0, 0.0).astype(F32)
    level_masks = []
    s = 1
    while s < CHUNK:
        level_masks.append((ri // (2 * s) == ci // (2 * s)) & ((ri // s) % 2 == 1) & ((ci // s) % 2 == 0))
        s *= 2

    cpi = min(CHUNKS_PER_ITER, nchunk)

    def chunk_body(ci, carry):
        ps = [(cc, h) for cc in range(cpi) for h in range(HEADS)]
        r0 = [pl.multiple_of((ci * cpi + cc) * CHUNK, CHUNK) for cc in range(cpi)]
        gcb = [g_ref[pl.ds(r0[cc], CHUNK), :] for cc in range(cpi)]
        gtb = [gt_ref[ci * cpi + cc] for cc in range(cpi)]
        beta_c = [gcb[cc][:, h:h + 1] for cc, h in ps]
        gc_c = [gcb[cc][:, HEADS + h:HEADS + h + 1] for cc, h in ps]
        gc_r = [gtb[cc][HEADS + h:HEADS + h + 1, :] for cc, h in ps]
        q = [act_ref[h, pl.ds(r0[cc], CHUNK), :] for cc, h in ps]
        k = [act_ref[HEADS + h, pl.ds(r0[cc], CHUNK), :] for cc, h in ps]
        v = [act_ref[2 * HEADS + h, pl.ds(r0[cc], CHUNK), :] for cc, h in ps]
        n = range(len(ps))

        decay = [jnp.exp(jnp.where(tril, gc_c[i] - gc_r[i], -jnp.inf)) for i in n]
        kq = [_mm_nt(jnp.concatenate([k[i], q[i]], axis=0), k[i]) for i in n]
        m = [beta_c[i] * kq[i][:CHUNK] * decay[i] for i in n]
        attn = [kq[i][CHUNK:] * decay[i] for i in n]

        x = [eye - jnp.where(level_masks[0], m[i], 0.0) for i in n]
        for lm in level_masks[1:]:
            pm = [_mm(jnp.where(lm, m[i], 0.0), x[i]) for i in n]
            x = [x[i] - _mm(x[i], pm[i]) for i in n]

        egc = [jnp.exp(gc_c[i]) for i in n]
        rhs = [jnp.concatenate([v[i] * beta_c[i], k[i] * (beta_c[i] * egc[i])], axis=1) for i in n]
        uw = [_mm(x[i], rhs[i]) for i in n]

        g_last = [gc_c[i][CHUNK - 1:CHUNK, :] for i in n]
        k_dec = [k[i] * jnp.exp(g_last[i] - gc_c[i]) for i in n]
        bg = [_mm_tn(k_dec[i], uw[i]) for i in n]
        ao = [_mm(attn[i], uw[i]) for i in n]
        gq = [jnp.concatenate([bg[i][:, HEAD_DIM:], q[i] * egc[i] - ao[i][:, HEAD_DIM:]], axis=0) for i in n]
        st = [s_ref[h] for h in range(HEADS)]
        for cc in range(cpi):
            idx = [cc * HEADS + h for h in range(HEADS)]
            r = [_mm(gq[i], st[h]) for h, i in enumerate(idx)]
            for h, i in enumerate(idx):
                o_ref[pl.ds(r0[cc], CHUNK), h * HEAD_DIM:(h + 1) * HEAD_DIM] = r[h][HEAD_DIM:] + ao[i][:, :HEAD_DIM]
            st = [st[h] * jnp.exp(g_last[i]) - r[h][:HEAD_DIM] + bg[i][:, :HEAD_DIM] for h, i in enumerate(idx)]
        for h in range(HEADS):
            s_ref[h] = st[h]
        return carry

    lax.fori_loop(0, nchunk // cpi, chunk_body, 0)

    parts = []
    for h in range(HEADS):
        o = o_ref[:, h * HEAD_DIM:(h + 1) * HEAD_DIM]
        z = z_ref[:, h * HEAD_DIM:(h + 1) * HEAD_DIM].astype(F32)
        on = o * lax.rsqrt(jnp.mean(o * o, axis=-1, keepdims=True) + EPS) * nw_ref[...]
        parts.append((on * _silu(z)).astype(BF16))
    og = jnp.concatenate(parts, axis=1)
    out_ref[...] = jnp.dot(og, wo_ref[...], preferred_element_type=F32).astype(out_ref.dtype)


def _delta_branch(p, ba, conv_w, alog, dtb, norm_w, w_o, *, batch, seq, tt):
    width = HEADS * HEAD_DIM
    nt = seq // tt
    d_out = w_o.shape[1]
    kern = functools.partial(_delta_kernel, tt=tt)
    row = lambda b, t: b * nt + t
    return pl.pallas_call(
        kern,
        grid=(batch, nt),
        in_specs=[
            pl.BlockSpec((tt, 3 * width), lambda b, t: (row(b, t), 0)),
            pl.BlockSpec((tt, width), lambda b, t: (row(b, t), 3)),
            pl.BlockSpec((tt, LANES), lambda b, t: (row(b, t), 0)),
            pl.BlockSpec((3 * HEADS, SUBLANES, HEAD_DIM), lambda b, t: (0, 0, 0)),
            pl.BlockSpec((1, LANES), lambda b, t: (0, 0)),
            pl.BlockSpec((1, LANES), lambda b, t: (0, 0)),
            pl.BlockSpec((1, HEAD_DIM), lambda b, t: (0, 0)),
            pl.BlockSpec((width, d_out), lambda b, t: (0, 0)),
        ],
        out_specs=pl.BlockSpec((tt, d_out), lambda b, t: (row(b, t), 0)),
        out_shape=jax.ShapeDtypeStruct((batch * seq, d_out), BF16),
        scratch_shapes=[
            pltpu.VMEM((3 * HEADS, SUBLANES + tt, HEAD_DIM), F32),
            pltpu.VMEM((3 * HEADS, tt, HEAD_DIM), F32),
            pltpu.VMEM((tt, LANES), F32),
            pltpu.VMEM((tt // CHUNK, LANES, CHUNK), F32),
            pltpu.VMEM((HEADS, HEAD_DIM, HEAD_DIM), F32),
            pltpu.VMEM((tt, width), F32),
        ],
        compiler_params=pltpu.CompilerParams(
            dimension_semantics=("arbitrary", "arbitrary"),
            vmem_limit_bytes=VMEM_LIMIT_BYTES),
        name="delta_branch",
    )(p, p, ba, conv_w, alog, dtb, norm_w, w_o)


def _convmod_kernel(glu_ref, bglu_ref, dw_ref, dwb_ref, lnw_ref, lnb_ref, w2_ref, b2_ref, out_ref,
                    x0_ref, x1_ref, xf_ref, y_ref, c_ref, *, tt):
    width = lnw_ref.shape[1]
    nslab = width // LANES
    hw = 2 * SUBLANES
    rb = 2 * SUBLANES
    group = 4
    t = pl.program_id(1)

    @pl.when(t == 0)
    def _():
        x0_ref[:, 0:hw, :] = jnp.zeros((nslab, hw, LANES), jnp.uint32)
        x1_ref[:, 0:hw, :] = jnp.zeros((nslab, hw, LANES), jnp.uint32)
        xf_ref[:, 0:SUBLANES, :] = jnp.zeros((nslab, SUBLANES, LANES), F32)

    for j in range(nslab):
        ca = slice(j * LANES, (j + 1) * LANES)
        cb = slice(width + j * LANES, width + (j + 1) * LANES)
        ga = glu_ref[:, ca].astype(F32) + bglu_ref[:, ca]
        gb = glu_ref[:, cb].astype(F32) + bglu_ref[:, cb]
        c = ga * _sigmoid(gb)
        xf_ref[j, SUBLANES:SUBLANES + tt, :] = c
        x0_ref[j, hw:hw + tt // 2, :] = pltpu.bitcast(c.astype(BF16), jnp.uint32)
        shifted = xf_ref[j, SUBLANES - 1:SUBLANES - 1 + tt, :]
        x1_ref[j, hw:hw + tt // 2, :] = pltpu.bitcast(shifted.astype(BF16), jnp.uint32)

    def conv_body(j, carry):
        for g0 in range(0, tt, group * rb):
            acc = [None] * group
            for k in range(CM_KERNEL):
                wk = dw_ref[j, k].astype(F32)
                for b in range(group):
                    off = g0 + b * rb - (CM_KERNEL - 1) + k
                    if off % 2 == 0:
                        words = x0_ref[j, hw + off // 2:hw + off // 2 + rb // 2, :]
                    else:
                        words = x1_ref[j, hw + (off + 1) // 2:hw + (off + 1) // 2 + rb // 2, :]
                    term = pltpu.bitcast(words, BF16).astype(F32) * wk
                    acc[b] = term if acc[b] is None else acc[b] + term
            for b in range(group):
                r0 = g0 + b * rb
                y_ref[j, r0:r0 + rb, :] = acc[b] + dwb_ref[j]
        return carry

    lax.fori_loop(0, nslab, conv_body, 0)
    x0_ref[:, 0:hw, :] = x0_ref[:, tt // 2:tt // 2 + hw, :]
    x1_ref[:, 0:hw, :] = x1_ref[:, tt // 2:tt // 2 + hw, :]
    xf_ref[:, 0:SUBLANES, :] = xf_ref[:, tt:tt + SUBLANES, :]

    tot = y_ref[0]
    for j in range(1, nslab):
        tot = tot + y_ref[j]
    mu = jnp.sum(tot, axis=-1, keepdims=True) * (1.0 / width)
    sq = None
    for j in range(nslab):
        cc = y_ref[j] - mu
        sq = cc * cc if sq is None else sq + cc * cc
    rstd = lax.rsqrt(jnp.sum(sq, axis=-1, keepdims=True) * (1.0 / width) + EPS)
    for j in range(nslab):
        ca = slice(j * LANES, (j + 1) * LANES)
        y = (y_ref[j] - mu) * rstd * lnw_ref[:, ca] + lnb_ref[:, ca]
        c_ref[:, ca] = _silu(y).astype(BF16)

    out = jnp.dot(c_ref[...], w2_ref[...], preferred_element_type=F32) + b2_ref[...]
    out_ref[...] = out.astype(out_ref.dtype)


def _conv_branch(p, b_glu, dw_taps, dw_bias, ln_w, ln_b, w_pw2, b_pw2, *, batch, seq, tt):
    width = ln_w.shape[1]
    nslab = width // LANES
    d_out = w_pw2.shape[1]
    nt = seq // tt
    kern = functools.partial(_convmod_kernel, tt=tt)
    row = lambda b, t: b * nt + t
    const = lambda b, t: (0, 0)
    return pl.pallas_call(
        kern,
        grid=(batch, nt),
        in_specs=[
            pl.BlockSpec((tt, 2 * width), lambda b, t: (row(b, t), 2)),
            pl.BlockSpec((1, 2 * width), const),
            pl.BlockSpec((nslab, CM_KERNEL, 2 * SUBLANES, LANES), lambda b, t: (0, 0, 0, 0)),
            pl.BlockSpec((nslab, 1, LANES), lambda b, t: (0, 0, 0)),
            pl.BlockSpec((1, width), const),
            pl.BlockSpec((1, width), const),
            pl.BlockSpec((width, d_out), const),
            pl.BlockSpec((1, d_out), const),
        ],
        out_specs=pl.BlockSpec((tt, d_out), lambda b, t: (row(b, t), 0)),
        out_shape=jax.ShapeDtypeStruct((batch * seq, d_out), BF16),
        scratch_shapes=[
            pltpu.VMEM((nslab, 2 * SUBLANES + tt // 2, LANES), jnp.uint32),
            pltpu.VMEM((nslab, 2 * SUBLANES + tt // 2, LANES), jnp.uint32),
            pltpu.VMEM((nslab, SUBLANES + tt, LANES), F32),
            pltpu.VMEM((nslab, tt, LANES), F32),
            pltpu.VMEM((tt, width), BF16),
        ],
        compiler_params=pltpu.CompilerParams(
            dimension_semantics=("arbitrary", "arbitrary"),
            vmem_limit_bytes=VMEM_LIMIT_BYTES),
        name="conv_branch",
    )(p, b_glu, dw_taps, dw_bias, ln_w, ln_b, w_pw2, b_pw2)


def _merge_ffn_kernel(x_ref, oa_ref, ob_ref, gates_ref, bg_ref, wout_ref, n2_ref, wgu_ref, wdn_ref,
                      nf_ref, out_ref):
    d = x_ref.shape[1]
    hidden = wdn_ref.shape[0]
    gate_a = _sigmoid(gates_ref[:, :d].astype(F32) + bg_ref[:, :d])
    gate_b = _sigmoid(gates_ref[:, d:].astype(F32) + bg_ref[:, d:])
    merged = gate_a * oa_ref[...].astype(F32) + gate_b * ob_ref[...].astype(F32)
    x1 = x_ref[...] + jnp.dot(merged.astype(BF16), wout_ref[...], preferred_element_type=F32)
    h2 = x1 * lax.rsqrt(jnp.mean(x1 * x1, axis=-1, keepdims=True) + EPS) * n2_ref[...]
    gu = jnp.dot(h2.astype(BF16), wgu_ref[...], preferred_element_type=F32)
    act = (_silu(gu[:, :hidden]) * gu[:, hidden:]).astype(BF16)
    x2 = x1 + jnp.dot(act, wdn_ref[...], preferred_element_type=F32)
    out_ref[...] = x2 * lax.rsqrt(jnp.mean(x2 * x2, axis=-1, keepdims=True) + EPS) * nf_ref[...]


def _merge_ffn(x2d, out_a, out_b, p, b_gates, w_out, norm2_w, w_gu, w_dn, norm_f_w, *, tm):
    n, d = x2d.shape
    hidden = w_dn.shape[0]
    const = lambda i: (0, 0)
    single = dict(pipeline_mode=pl.Buffered(1))
    return pl.pallas_call(
        _merge_ffn_kernel,
        grid=(n // tm,),
        in_specs=[
            pl.BlockSpec((tm, d), lambda i: (i, 0)),
            pl.BlockSpec((tm, d), lambda i: (i, 0)),
            pl.BlockSpec((tm, d), lambda i: (i, 0)),
            pl.BlockSpec((tm, 2 * d), lambda i: (i, 3)),
            pl.BlockSpec((1, 2 * d), const),
            pl.BlockSpec((d, d), const, **single),
            pl.BlockSpec((1, d), const),
            pl.BlockSpec((d, 2 * hidden), const, **single),
            pl.BlockSpec((hidden, d), const, **single),
            pl.BlockSpec((1, d), const),
        ],
        out_specs=pl.BlockSpec((tm, d), lambda i: (i, 0)),
        out_shape=jax.ShapeDtypeStruct((n, d), F32),
        compiler_params=pltpu.CompilerParams(
            dimension_semantics=("arbitrary",),
            vmem_limit_bytes=VMEM_LIMIT_BYTES),
        name="merge_ffn",
    )(x2d, out_a, out_b, p, b_gates, w_out, norm2_w, w_gu, w_dn, norm_f_w)


def _pick_tile(n, target):
    t = min(n, target)
    while n % t:
        t //= 2
    return t


def kernel(x, norm1_w, w_in, b_glu, b_gates, dn_conv_w, dn_A_log, dn_dt_bias, dn_norm_w, dn_w_o,
           cm_dw_w, cm_dw_b, cm_ln_w, cm_ln_b, cm_w_pw2, cm_b_pw2, w_out, norm2_w, ffn_w_gate_up,
           ffn_w_down, norm_f_w):
    batch, seq, d = x.shape
    depth = w_in.shape[0]
    width = HEADS * HEAD_DIM
    assert d == width and cm_ln_w.shape[1] == d and seq % CHUNK == 0
    n = batch * seq
    tt = _pick_tile(seq, 512)
    xf = x.reshape(n, d)

    for l in range(depth):
        wl = w_in[l]
        w_main = jnp.concatenate([wl[:, :4 * width], wl[:, 4 * width + 2 * HEADS:]], axis=1).astype(BF16)
        w_ba = jnp.pad(wl[:, 4 * width:4 * width + 2 * HEADS], ((0, 0), (0, LANES - 2 * HEADS)))
        w_ba_hi = w_ba.astype(BF16)
        w_ba_lo = (w_ba - w_ba_hi.astype(F32)).astype(BF16)
        w_ba2 = jnp.concatenate([w_ba_hi, w_ba_lo], axis=1)
        lane_pad = lambda v: jnp.pad(v.reshape(1, HEADS), ((0, 0), (HEADS, LANES - 2 * HEADS)))

        p, ba = _inproj(xf, norm1_w[l].reshape(1, d), w_main, w_ba2,
                        tm=_pick_tile(n, 512), tn=w_main.shape[1])
        dn_cw = jnp.pad(dn_conv_w[l].reshape(DN_CONV, 3 * HEADS, HEAD_DIM).transpose(1, 0, 2),
                        ((0, 0), (0, SUBLANES - DN_CONV), (0, 0)))
        cm_taps = cm_dw_w[l].reshape(CM_KERNEL, d // LANES, LANES).transpose(1, 0, 2).astype(BF16)
        cm_taps = jnp.broadcast_to(cm_taps[:, :, None, :], (d // LANES, CM_KERNEL, 2 * SUBLANES, LANES))
        cm_bias = cm_dw_b[l].reshape(d // LANES, 1, LANES)

        out_a = _delta_branch(p, ba, dn_cw, lane_pad(dn_A_log[l]), lane_pad(dn_dt_bias[l]),
                              dn_norm_w[l].reshape(1, HEAD_DIM), dn_w_o[l].astype(BF16),
                              batch=batch, seq=seq, tt=tt)
        out_b = _conv_branch(p, b_glu[l].reshape(1, 2 * d), cm_taps, cm_bias,
                             cm_ln_w[l].reshape(1, d), cm_ln_b[l].reshape(1, d),
                             cm_w_pw2[l].astype(BF16), cm_b_pw2[l].reshape(1, d),
                             batch=batch, seq=seq, tt=tt)
        assert depth == 1
        xf = _merge_ffn(xf, out_a, out_b, p, b_gates[l].reshape(1, 2 * d), w_out[l].astype(BF16),
                        norm2_w[l].reshape(1, d), ffn_w_gate_up[l].astype(BF16),
                        ffn_w_down[l].astype(BF16), norm_f_w.reshape(1, d), tm=_pick_tile(n, 512))
    return xf.reshape(batch, seq, d)
```

```python
import functools

import jax
import jax.numpy as jnp
from jax import lax
from jax.experimental import pallas as pl
from jax.experimental.pallas import tpu as pltpu

F32 = jnp.float32
BF16 = jnp.bfloat16

EPS = 1e-6
CHUNK = 64
HEADS = 8
HEAD_DIM = 128
DN_CONV = 4
CM_KERNEL = 31
LANES = 128
SUBLANES = 8
VMEM_LIMIT_BYTES = 56 * 1024 * 1024
SUBTILES = 2
CHUNKS_PER_ITER = 4


def _mm(a, b):
    return jnp.dot(a.astype(BF16), b.astype(BF16), preferred_element_type=F32)


def _mm_nt(a, b):
    return lax.dot_general(a.astype(BF16), b.astype(BF16), (((1,), (1,)), ((), ())),
                           preferred_element_type=F32)


def _mm_tn(a, b):
    return lax.dot_general(a.astype(BF16), b.astype(BF16), (((0,), (0,)), ((), ())),
                           preferred_element_type=F32)


def _split(a):
    hi = a.astype(BF16)
    lo = (a - hi.astype(F32)).astype(BF16)
    return hi, lo


def _sigmoid(x):
    return 1.0 / (1.0 + jnp.exp(-x))


def _silu(x):
    return x * _sigmoid(x)


def _inproj_kernel(x_ref, nw_ref, w_ref, wba_ref, p_ref, ba_ref, h_ref):
    j = pl.program_id(1)

    @pl.when(j == 0)
    def _():
        x = x_ref[...]
        ms = jnp.mean(x * x, axis=-1, keepdims=True)
        h = x * lax.rsqrt(ms + EPS) * nw_ref[...]
        hh, hl = _split(h)
        h_ref[...] = hh
        r = jnp.dot(hh, wba_ref[...], preferred_element_type=F32)
        r2 = jnp.dot(hl, wba_ref[:, :LANES], preferred_element_type=F32)
        ba_ref[...] = r[:, :LANES] + (r[:, LANES:] + r2)

    p_ref[...] = jnp.dot(h_ref[...], w_ref[...], preferred_element_type=F32).astype(p_ref.dtype)


def _inproj(x2, norm_w, w_main, w_ba, *, tm, tn):
    n, d = x2.shape
    cols = w_main.shape[1]
    return pl.pallas_call(
        _inproj_kernel,
        grid=(n // tm, cols // tn),
        in_specs=[
            pl.BlockSpec((tm, d), lambda i, j: (i, 0)),
            pl.BlockSpec((1, d), lambda i, j: (0, 0)),
            pl.BlockSpec((d, tn), lambda i, j: (0, j),
                         **(dict(pipeline_mode=pl.Buffered(1)) if tn == cols else {})),
            pl.BlockSpec((d, 2 * LANES), lambda i, j: (0, 0)),
        ],
        out_specs=[
            pl.BlockSpec((tm, tn), lambda i, j: (i, j)),
            pl.BlockSpec((tm, LANES), lambda i, j: (i, 0)),
        ],
        out_shape=[
            jax.ShapeDtypeStruct((n, cols), BF16),
            jax.ShapeDtypeStruct((n, LANES), F32),
        ],
        scratch_shapes=[pltpu.VMEM((tm, d), BF16)],
        compiler_params=pltpu.CompilerParams(
            dimension_semantics=("arbitrary", "arbitrary"),
            vmem_limit_bytes=VMEM_LIMIT_BYTES),
        name="inproj",
    )(x2, norm_w, w_main, w_ba)


def _delta_kernel(qkv_ref, z_ref, ba_ref, cw_ref, alog_ref, dtb_ref, nw_ref, wo_ref, out_ref,
                  xh_ref, act_ref, g_ref, gt_ref, s_ref, o_ref, *, tt):
    width = HEADS * HEAD_DIM
    nchunk = tt // CHUNK
    nslab = 3 * HEADS
    hist = SUBLANES
    t = pl.program_id(1)

    @pl.when(t == 0)
    def _():
        xh_ref[:, 0:hist, :] = jnp.zeros((nslab, hist, HEAD_DIM), F32)
        s_ref[...] = jnp.zeros_like(s_ref)

    for j in range(nslab):
        xh_ref[j, hist:hist + tt, :] = qkv_ref[:, j * HEAD_DIM:(j + 1) * HEAD_DIM].astype(F32)

    rb = 4 * SUBLANES

    def conv_slab(j, post):
        for r0 in range(0, tt, rb):
            acc = None
            for k in range(DN_CONV):
                lo = r0 + hist - (DN_CONV - 1) + k
                term = xh_ref[j, lo:lo + rb, :] * cw_ref[j, k:k + 1, :]
                acc = term if acc is None else acc + term
            y = _silu(acc)
            if post is not None:
                y = y * (lax.rsqrt(jnp.sum(y * y, axis=-1, keepdims=True) + EPS) * post)
            act_ref[j, r0:r0 + rb, :] = y

    def qk_body(j, carry):
        conv_slab(j, jnp.where(j < HEADS, HEAD_DIM ** -0.5, 1.0).astype(F32))
        return carry

    def v_body(j, carry):
        conv_slab(j, None)
        return carry

    lax.fori_loop(0, 2 * HEADS, qk_body, 0, unroll=4)
    lax.fori_loop(2 * HEADS, nslab, v_body, 0, unroll=4)
    xh_ref[:, 0:hist, :] = xh_ref[:, tt:tt + hist, :]

    ba = ba_ref[...]
    lane = lax.broadcasted_iota(jnp.int32, (tt, LANES), 1)
    beta = _sigmoid(ba)
    sp_in = ba + dtb_ref[...]
    softplus = jnp.maximum(sp_in, 0.0) + jnp.log1p(jnp.exp(-jnp.abs(sp_in)))
    g = -jnp.exp(alog_ref[...]) * softplus
    g = jnp.where((lane >= HEADS) & (lane < 2 * HEADS), g, 0.0)
    row = lax.broadcasted_iota(jnp.int32, (tt, tt), 0)
    col = lax.broadcasted_iota(jnp.int32, (tt, tt), 1)
    tri = jnp.where((row // CHUNK == col // CHUNK) & (row >= col), 1.0, 0.0).astype(BF16)
    gh, gl = _split(g)
    gc = jnp.dot(tri, gh, preferred_element_type=F32) + jnp.dot(tri, gl, preferred_element_type=F32)
    gfull = jnp.where(lane < HEADS, beta, gc)
    g_ref[...] = gfull
    gt = gfull.T
    for c in range(nchunk):
        gt_ref[c] = gt[:, c * CHUNK:(c + 1) * CHUNK]

    ri = lax.broadcasted_iota(jnp.int32, (CHUNK, CHUNK), 0)
    ci = lax.broadcasted_iota(jnp.int32, (CHUNK, CHUNK), 1)
    tril = ri >= ci
    eye = jnp.where(ri == ci, 1.0, 0.0).astype(F32)
    level_masks = []
    s = 1
    while s < CHUNK:
        level_masks.append((ri // (2 * s) == ci // (2 * s)) & ((ri // s) % 2 == 1) & ((ci // s) % 2 == 0))
        s *= 2

    cpi = min(CHUNKS_PER_ITER, nchunk)

    def chunk_body(ci, carry):
        ps = [(cc, h) for cc in range(cpi) for h in range(HEADS)]
        r0 = [pl.multiple_of((ci * cpi + cc) * CHUNK, CHUNK) for cc in range(cpi)]
        gcb = [g_ref[pl.ds(r0[cc], CHUNK), :] for cc in range(cpi)]
        gtb = [gt_ref[ci * cpi + cc] for cc in range(cpi)]
        beta_c = [gcb[cc][:, h:h + 1] for cc, h in ps]
        gc_c = [gcb[cc][:, HEADS + h:HEADS + h + 1] for cc, h in ps]
        gc_r = [gtb[cc][HEADS + h:HEADS + h + 1, :] for cc, h in ps]
        q = [act_ref[h, pl.ds(r0[cc], CHUNK), :] for cc, h in ps]
        k = [act_ref[HEADS + h, pl.ds(r0[cc], CHUNK), :] for cc, h in ps]
        v = [act_ref[2 * HEADS + h, pl.ds(r0[cc], CHUNK), :] for cc, h in ps]
        n = range(len(ps))

        decay = [jnp.exp(jnp.where(tril, gc_c[i] - gc_r[i], -jnp.inf)) for i in n]
        kq = [_mm_nt(jnp.concatenate([k[i], q[i]], axis=0), k[i]) for i in n]
        m = [beta_c[i] * kq[i][:CHUNK] * decay[i] for i in n]
        attn = [kq[i][CHUNK:] * decay[i] for i in n]

        x = [eye - jnp.where(level_masks[0], m[i], 0.0) for i in n]
        for lm in level_masks[1:]:
            pm = [_mm(jnp.where(lm, m[i], 0.0), x[i]) for i in n]
            x = [x[i] - _mm(x[i], pm[i]) for i in n]

        egc = [jnp.exp(gc_c[i]) for i in n]
        rhs = [jnp.concatenate([v[i] * beta_c[i], k[i] * (beta_c[i] * egc[i])], axis=1) for i in n]
        uw = [_mm(x[i], rhs[i]) for i in n]

        g_last = [gc_c[i][CHUNK - 1:CHUNK, :] for i in n]
        k_dec = [k[i] * jnp.exp(g_last[i] - gc_c[i]) for i in n]
        bg = [_mm_tn(k_dec[i], uw[i]) for i in n]
        ao = [_mm(attn[i], uw[i]) for i in n]
        gq = [jnp.concatenate([bg[i][:, HEAD_DIM:], q[i] * egc[i] - ao[i][:, HEAD_DIM:]], axis=0) for i in n]
        st = [s_ref[h] for h in range(HEADS)]
        for cc in range(cpi):
            idx = [cc * HEADS + h for h in range(HEADS)]
            r = [_mm(gq[i], st[h]) for h, i in enumerate(idx)]
            for h, i in enumerate(idx):
                o_ref[pl.ds(r0[cc], CHUNK), h * HEAD_DIM:(h + 1) * HEAD_DIM] = r[h][HEAD_DIM:] + ao[i][:, :HEAD_DIM]
            st = [st[h] * jnp.exp(g_last[i]) - r[h][:HEAD_DIM] + bg[i][:, :HEAD_DIM] for h, i in enumerate(idx)]
        for h in range(HEADS):
            s_ref[h] = st[h]
        return carry

    lax.fori_loop(0, nchunk // cpi, chunk_body, 0)

    parts = []
    for h in range(HEADS):
        o = o_ref[:, h * HEAD_DIM:(h + 1) * HEAD_DIM]
        z = z_ref[:, h * HEAD_DIM:(h + 1) * HEAD_DIM].astype(F32)
        on = o * lax.rsqrt(jnp.mean(o * o, axis=-1, keepdims=True) + EPS) * nw_ref[...]
        parts.append((on * _silu(z)).astype(BF16))
    og = jnp.concatenate(parts, axis=1)
    out_ref[...] = jnp.dot(og, wo_ref[...], preferred_element_type=F32).astype(out_ref.dtype)


def _delta_branch(p, ba, conv_w, alog, dtb, norm_w, w_o, *, batch, seq, tt):
    width = HEADS * HEAD_DIM
    nt = seq // tt
    d_out = w_o.shape[1]
    kern = functools.partial(_delta_kernel, tt=tt)
    row = lambda b, t: b * nt + t
    return pl.pallas_call(
        kern,
        grid=(batch, nt),
        in_specs=[
            pl.BlockSpec((tt, 3 * width), lambda b, t: (row(b, t), 0)),
            pl.BlockSpec((tt, width), lambda b, t: (row(b, t), 3)),
            pl.BlockSpec((tt, LANES), lambda b, t: (row(b, t), 0)),
            pl.BlockSpec((3 * HEADS, SUBLANES, HEAD_DIM), lambda b, t: (0, 0, 0)),
            pl.BlockSpec((1, LANES), lambda b, t: (0, 0)),
            pl.BlockSpec((1, LANES), lambda b, t: (0, 0)),
            pl.BlockSpec((1, HEAD_DIM), lambda b, t: (0, 0)),
            pl.BlockSpec((width, d_out), lambda b, t: (0, 0)),
        ],
        out_specs=pl.BlockSpec((tt, d_out), lambda b, t: (row(b, t), 0)),
        out_shape=jax.ShapeDtypeStruct((batch * seq, d_out), BF16),
        scratch_shapes=[
            pltpu.VMEM((3 * HEADS, SUBLANES + tt, HEAD_DIM), F32),
            pltpu.VMEM((3 * HEADS, tt, HEAD_DIM), F32),
            pltpu.VMEM((tt, LANES), F32),
            pltpu.VMEM((tt // CHUNK, LANES, CHUNK), F32),
            pltpu.VMEM((HEADS, HEAD_DIM, HEAD_DIM), F32),
            pltpu.VMEM((tt, width), F32),
        ],
        compiler_params=pltpu.CompilerParams(
            dimension_semantics=("arbitrary", "arbitrary"),
            vmem_limit_bytes=VMEM_LIMIT_BYTES),
        name="delta_branch",
    )(p, p, ba, conv_w, alog, dtb, norm_w, w_o)


def _convmod_kernel(glu_ref, bglu_ref, dw_ref, dwb_ref, lnw_ref, lnb_ref, w2_ref, b2_ref, out_ref,
                    x0_ref, x1_ref, xf_ref, y_ref, c_ref, *, tt):
    width = lnw_ref.shape[1]
    nslab = width // LANES
    hw = 2 * SUBLANES
    rb = 2 * SUBLANES
    group = 4
    t = pl.program_id(1)

    @pl.when(t == 0)
    def _():
        x0_ref[:, 0:hw, :] = jnp.zeros((nslab, hw, LANES), jnp.uint32)
        x1_ref[:, 0:hw, :] = jnp.zeros((nslab, hw, LANES), jnp.uint32)
        xf_ref[:, 0:SUBLANES, :] = jnp.zeros((nslab, SUBLANES, LANES), F32)

    for j in range(nslab):
        ca = slice(j * LANES, (j + 1) * LANES)
        cb = slice(width + j * LANES, width + (j + 1) * LANES)
        ga = glu_ref[:, ca].astype(F32) + bglu_ref[:, ca]
        gb = glu_ref[:, cb].astype(F32) + bglu_ref[:, cb]
        c = ga * _sigmoid(gb)
        xf_ref[j, SUBLANES:SUBLANES + tt, :] = c
        x0_ref[j, hw:hw + tt // 2, :] = pltpu.bitcast(c.astype(BF16), jnp.uint32)
        shifted = xf_ref[j, SUBLANES - 1:SUBLANES - 1 + tt, :]
        x1_ref[j, hw:hw + tt // 2, :] = pltpu.bitcast(shifted.astype(BF16), jnp.uint32)

    def conv_body(j, carry):
        for g0 in range(0, tt, group * rb):
            acc = [None] * group
            for k in range(CM_KERNEL):
                wk = dw_ref[j, k].astype(F32)
                for b in range(group):
                    off = g0 + b * rb - (CM_KERNEL - 1) + k
                    if off % 2 == 0:
                        words = x0_ref[j, hw + off // 2:hw + off // 2 + rb // 2, :]
                    else:
                        words = x1_ref[j, hw + (off + 1) // 2:hw + (off + 1) // 2 + rb // 2, :]
                    term = pltpu.bitcast(words, BF16).astype(F32) * wk
                    acc[b] = term if acc[b] is None else acc[b] + term
            for b in range(group):
                r0 = g0 + b * rb
                y_ref[j, r0:r0 + rb, :] = acc[b] + dwb_ref[j]
        return carry

    lax.fori_loop(0, nslab, conv_body, 0)
    x0_ref[:, 0:hw, :] = x0_ref[:, tt // 2:tt // 2 + hw, :]
    x1_ref[:, 0:hw, :] = x1_ref[:, tt // 2:tt // 2 + hw, :]
    xf_ref[:, 0:SUBLANES, :] = xf_ref[:, tt:tt + SUBLANES, :]

    tot = y_ref[0]
    for j in range(1, nslab):
        tot = tot + y_ref[j]
    mu = jnp.sum(tot, axis=-1, keepdims=True) * (1.0 / width)
    sq = None
    for j in range(nslab):
        cc = y_ref[j] - mu
        sq = cc * cc if sq is None else sq + cc * cc
    rstd = lax.rsqrt(jnp.sum(sq, axis=-1, keepdims=True) * (1.0 / width) + EPS)
    for j in range(nslab):
        ca = slice(j * LANES, (j + 1) * LANES)
        y = (y_ref[j] - mu) * rstd * lnw_ref[:, ca] + lnb_ref[:, ca]
        c_ref[:, ca] = _silu(y).astype(BF16)

    out = jnp.dot(c_ref[...], w2_ref[...], preferred_element_type=F32) + b2_ref[...]
    out_ref[...] = out.astype(out_ref.dtype)


def _conv_branch(p, b_glu, dw_taps, dw_bias, ln_w, ln_b, w_pw2, b_pw2, *, batch, seq, tt):
    width = ln_w.shape[1]
    nslab = width // LANES
    d_out = w_pw2.shape[1]
    nt = seq // tt
    kern = functools.partial(_convmod_kernel, tt=tt)
    row = lambda b, t: b * nt + t
    const = lambda b, t: (0, 0)
    return pl.pallas_call(
        kern,
        grid=(batch, nt),
        in_specs=[
            pl.BlockSpec((tt, 2 * width), lambda b, t: (row(b, t), 2)),
            pl.BlockSpec((1, 2 * width), const),
            pl.BlockSpec((nslab, CM_KERNEL, 2 * SUBLANES, LANES), lambda b, t: (0, 0, 0, 0)),
            pl.BlockSpec((nslab, 1, LANES), lambda b, t: (0, 0, 0)),
            pl.BlockSpec((1, width), const),
            pl.BlockSpec((1, width), const),
            pl.BlockSpec((width, d_out), const),
            pl.BlockSpec((1, d_out), const),
        ],
        out_specs=pl.BlockSpec((tt, d_out), lambda b, t: (row(b, t), 0)),
        out_shape=jax.ShapeDtypeStruct((batch * seq, d_out), BF16),
        scratch_shapes=[
            pltpu.VMEM((nslab, 2 * SUBLANES + tt // 2, LANES), jnp.uint32),
            pltpu.VMEM((nslab, 2 * SUBLANES + tt // 2, LANES), jnp.uint32),
            pltpu.VMEM((nslab, SUBLANES + tt, LANES), F32),
            pltpu.VMEM((nslab, tt, LANES), F32),
            pltpu.VMEM((tt, width), BF16),
        ],
        compiler_params=pltpu.CompilerParams(
            dimension_semantics=("arbitrary", "arbitrary"),
            vmem_limit_bytes=VMEM_LIMIT_BYTES),
        name="conv_branch",
    )(p, b_glu, dw_taps, dw_bias, ln_w, ln_b, w_pw2, b_pw2)


def _merge_ffn_kernel(x_ref, oa_ref, ob_ref, gates_ref, bg_ref, wout_ref, n2_ref, wgu_ref, wdn_ref,
                      nf_ref, out_ref):
    d = x_ref.shape[1]
    hidden = wdn_ref.shape[0]
    tm = x_ref.shape[0]
    nsub = SUBTILES if tm % (SUBTILES * 2 * SUBLANES) == 0 else 1
    rows = [slice(i * (tm // nsub), (i + 1) * (tm // nsub)) for i in range(nsub)]
    merged = []
    for r in rows:
        gate_a = _sigmoid(gates_ref[r, :d].astype(F32) + bg_ref[:, :d])
        gate_b = _sigmoid(gates_ref[r, d:].astype(F32) + bg_ref[:, d:])
        merged.append((gate_a * oa_ref[r, :].astype(F32) + gate_b * ob_ref[r, :].astype(F32)).astype(BF16))
    x1 = [x_ref[r, :] + jnp.dot(m, wout_ref[...], preferred_element_type=F32) for r, m in zip(rows, merged)]
    h2 = [(v * lax.rsqrt(jnp.mean(v * v, axis=-1, keepdims=True) + EPS) * n2_ref[...]).astype(BF16) for v in x1]
    gu = [jnp.dot(h, wgu_ref[...], preferred_element_type=F32) for h in h2]
    act = [(_silu(g[:, :hidden]) * g[:, hidden:]).astype(BF16) for g in gu]
    x2 = [v + jnp.dot(a, wdn_ref[...], preferred_element_type=F32) for v, a in zip(x1, act)]
    for r, v in zip(rows, x2):
        out_ref[r, :] = v * lax.rsqrt(jnp.mean(v * v, axis=-1, keepdims=True) + EPS) * nf_ref[...]


def _merge_ffn(x2d, out_a, out_b, p, b_gates, w_out, norm2_w, w_gu, w_dn, norm_f_w, *, tm):
    n, d = x2d.shape
    hidden = w_dn.shape[0]
    const = lambda i: (0, 0)
    single = dict(pipeline_mode=pl.Buffered(1))
    return pl.pallas_call(
        _merge_ffn_kernel,
        grid=(n // tm,),
        in_specs=[
            pl.BlockSpec((tm, d), lambda i: (i, 0)),
            pl.BlockSpec((tm, d), lambda i: (i, 0)),
            pl.BlockSpec((tm, d), lambda i: (i, 0)),
            pl.BlockSpec((tm, 2 * d), lambda i: (i, 3)),
            pl.BlockSpec((1, 2 * d), const),
            pl.BlockSpec((d, d), const, **single),
            pl.BlockSpec((1, d), const),
            pl.BlockSpec((d, 2 * hidden), const, **single),
            pl.BlockSpec((hidden, d), const, **single),
            pl.BlockSpec((1, d), const),
        ],
        out_specs=pl.BlockSpec((tm, d), lambda i: (i, 0)),
        out_shape=jax.ShapeDtypeStruct((n, d), F32),
        compiler_params=pltpu.CompilerParams(
            dimension_semantics=("arbitrary",),
            vmem_limit_bytes=VMEM_LIMIT_BYTES),
        name="merge_ffn",
    )(x2d, out_a, out_b, p, b_gates, w_out, norm2_w, w_gu, w_dn, norm_f_w)


def _pick_tile(n, target):
    t = min(n, target)
    while n % t:
        t //= 2
    return t


def kernel(x, norm1_w, w_in, b_glu, b_gates, dn_conv_w, dn_A_log, dn_dt_bias, dn_norm_w, dn_w_o,
           cm_dw_w, cm_dw_b, cm_ln_w, cm_ln_b, cm_w_pw2, cm_b_pw2, w_out, norm2_w, ffn_w_gate_up,
           ffn_w_down, norm_f_w):
    batch, seq, d = x.shape
    depth = w_in.shape[0]
    width = HEADS * HEAD_DIM
    assert d == width and cm_ln_w.shape[1] == d and seq % CHUNK == 0
    n = batch * seq
    tt = _pick_tile(seq, 512)
    xf = x.reshape(n, d)

    for l in range(depth):
        wl = w_in[l]
        w_main = jnp.concatenate([wl[:, :4 * width], wl[:, 4 * width + 2 * HEADS:]], axis=1).astype(BF16)
        w_ba = jnp.pad(wl[:, 4 * width:4 * width + 2 * HEADS], ((0, 0), (0, LANES - 2 * HEADS)))
        w_ba_hi = w_ba.astype(BF16)
        w_ba_lo = (w_ba - w_ba_hi.astype(F32)).astype(BF16)
        w_ba2 = jnp.concatenate([w_ba_hi, w_ba_lo], axis=1)
        lane_pad = lambda v: jnp.pad(v.reshape(1, HEADS), ((0, 0), (HEADS, LANES - 2 * HEADS)))

        p, ba = _inproj(xf, norm1_w[l].reshape(1, d), w_main, w_ba2,
                        tm=_pick_tile(n, 512), tn=w_main.shape[1])
        dn_cw = jnp.pad(dn_conv_w[l].reshape(DN_CONV, 3 * HEADS, HEAD_DIM).transpose(1, 0, 2),
                        ((0, 0), (0, SUBLANES - DN_CONV), (0, 0)))
        cm_taps = cm_dw_w[l].reshape(CM_KERNEL, d // LANES, LANES).transpose(1, 0, 2).astype(BF16)
        cm_taps = jnp.broadcast_to(cm_taps[:, :, None, :], (d // LANES, CM_KERNEL, 2 * SUBLANES, LANES))
        cm_bias = cm_dw_b[l].reshape(d // LANES, 1, LANES)

        out_a = _delta_branch(p, ba, dn_cw, lane_pad(dn_A_log[l]), lane_pad(dn_dt_bias[l]),
                              dn_norm_w[l].reshape(1, HEAD_DIM), dn_w_o[l].astype(BF16),
                              batch=batch, seq=seq, tt=tt)
        out_b = _conv_branch(p, b_glu[l].reshape(1, 2 * d), cm_taps, cm_bias,
                             cm_ln_w[l].reshape(1, d), cm_ln_b[l].reshape(1, d),
                             cm_w_pw2[l].astype(BF16), cm_b_pw2[l].reshape(1, d),
                             batch=batch, seq=seq, tt=tt)
        assert depth == 1
        xf = _merge_ffn(xf, out_a, out_b, p, b_gates[l].reshape(1, 2 * d), w_out[l].astype(BF16),
                        norm2_w[l].reshape(1, d), ffn_w_gate_up[l].astype(BF16),
                        ffn_w_down[l].astype(BF16), norm_f_w.reshape(1, d), tm=_pick_tile(n, 512))
    return xf.reshape(batch, seq, d)
```

```python
import functools

import jax
import jax.numpy as jnp
from jax import lax
from jax.experimental import pallas as pl
from jax.experimental.pallas import tpu as pltpu

F32 = jnp.float32
BF16 = jnp.bfloat16

EPS = 1e-6
CHUNK = 64
HEADS = 8
HEAD_DIM = 128
DN_CONV = 4
CM_KERNEL = 31
LANES = 128
SUBLANES = 8
VMEM_LIMIT_BYTES = 56 * 1024 * 1024
SUBTILES = 2
CHUNKS_PER_ITER = 4


def _mm(a, b):
    return jnp.dot(a.astype(BF16), b.astype(BF16), preferred_element_type=F32)


def _mm_nt(a, b):
    return lax.dot_general(a.astype(BF16), b.astype(BF16), (((1,), (1,)), ((), ())),
                           preferred_element_type=F32)


def _mm_tn(a, b):
    return lax.dot_general(a.astype(BF16), b.astype(BF16), (((0,), (0,)), ((), ())),
                           preferred_element_type=F32)


def _split(a):
    hi = a.astype(BF16)
    lo = (a - hi.astype(F32)).astype(BF16)
    return hi, lo


def _sigmoid(x):
    return 1.0 / (1.0 + jnp.exp(-x))


def _silu(x):
    return x * _sigmoid(x)


def _inproj_kernel(x_ref, nw_ref, w_ref, wba_ref, p_ref, ba_ref, h_ref):
    j = pl.program_id(1)

    @pl.when(j == 0)
    def _():
        x = x_ref[...]
        ms = jnp.mean(x * x, axis=-1, keepdims=True)
        h = x * lax.rsqrt(ms + EPS) * nw_ref[...]
        hh, hl = _split(h)
        h_ref[...] = hh
        r = jnp.dot(hh, wba_ref[...], preferred_element_type=F32)
        r2 = jnp.dot(hl, wba_ref[:, :LANES], preferred_element_type=F32)
        ba_ref[...] = r[:, :LANES] + (r[:, LANES:] + r2)

    p_ref[...] = jnp.dot(h_ref[...], w_ref[...], preferred_element_type=F32).astype(p_ref.dtype)


def _inproj(x2, norm_w, w_main, w_ba, *, tm, tn):
    n, d = x2.shape
    cols = w_main.shape[1]
    return pl.pallas_call(
        _inproj_kernel,
        grid=(n // tm, cols // tn),
        in_specs=[
            pl.BlockSpec((tm, d), lambda i, j: (i, 0)),
            pl.BlockSpec((1, d), lambda i, j: (0, 0)),
            pl.BlockSpec((d, tn), lambda i, j: (0, j),
                         **(dict(pipeline_mode=pl.Buffered(1)) if tn == cols else {})),
            pl.BlockSpec((d, 2 * LANES), lambda i, j: (0, 0)),
        ],
        out_specs=[
            pl.BlockSpec((tm, tn), lambda i, j: (i, j)),
            pl.BlockSpec((tm, LANES), lambda i, j: (i, 0)),
        ],
        out_shape=[
            jax.ShapeDtypeStruct((n, cols), BF16),
            jax.ShapeDtypeStruct((n, LANES), F32),
        ],
        scratch_shapes=[pltpu.VMEM((tm, d), BF16)],
        compiler_params=pltpu.CompilerParams(
            dimension_semantics=("arbitrary", "arbitrary"),
            vmem_limit_bytes=VMEM_LIMIT_BYTES),
        name="inproj",
    )(x2, norm_w, w_main, w_ba)


def _delta_kernel(qkv_ref, z_ref, ba_ref, cw_ref, alog_ref, dtb_ref, nw_ref, wo_ref, out_ref,
                  xh_ref, act_ref, g_ref, gt_ref, s_ref, o_ref, *, tt):
    width = HEADS * HEAD_DIM
    nchunk = tt // CHUNK
    nslab = 3 * HEADS
    hist = SUBLANES
    t = pl.program_id(1)

    @pl.when(t == 0)
    def _():
        xh_ref[:, 0:hist, :] = jnp.zeros((nslab, hist, HEAD_DIM), F32)
        s_ref[...] = jnp.zeros_like(s_ref)

    for j in range(nslab):
        xh_ref[j, hist:hist + tt, :] = qkv_ref[:, j * HEAD_DIM:(j + 1) * HEAD_DIM].astype(F32)

    rb = 4 * SUBLANES

    def conv_slab(j, post):
        for r0 in range(0, tt, rb):
            acc = None
            for k in range(DN_CONV):
                lo = r0 + hist - (DN_CONV - 1) + k
                term = xh_ref[j, lo:lo + rb, :] * cw_ref[j, k:k + 1, :]
                acc = term if acc is None else acc + term
            y = _silu(acc)
            if post is not None:
                y = y * (lax.rsqrt(jnp.sum(y * y, axis=-1, keepdims=True) + EPS) * post)
            act_ref[j, r0:r0 + rb, :] = y

    def qk_body(j, carry):
        conv_slab(j, jnp.where(j < HEADS, HEAD_DIM ** -0.5, 1.0).astype(F32))
        return carry

    def v_body(j, carry):
        conv_slab(j, None)
        return carry

    lax.fori_loop(0, 2 * HEADS, qk_body, 0, unroll=4)
    lax.fori_loop(2 * HEADS, nslab, v_body, 0, unroll=4)
    xh_ref[:, 0:hist, :] = xh_ref[:, tt:tt + hist, :]

    ba = ba_ref[...]
    lane = lax.broadcasted_iota(jnp.int32, (tt, LANES), 1)
    beta = _sigmoid(ba)
    sp_in = ba + dtb_ref[...]
    softplus = jnp.maximum(sp_in, 0.0) + jnp.log1p(jnp.exp(-jnp.abs(sp_in)))
    g = -jnp.exp(alog_ref[...]) * softplus
    g = jnp.where((lane >= HEADS) & (lane < 2 * HEADS), g, 0.0)
    row = lax.broadcasted_iota(jnp.int32, (tt, tt), 0)
    col = lax.broadcasted_iota(jnp.int32, (tt, tt), 1)
    tri = jnp.where((row // CHUNK == col // CHUNK) & (row >= col), 1.0, 0.0).astype(BF16)
    gh, gl = _split(g)
    gc = jnp.dot(tri, gh, preferred_element_type=F32) + jnp.dot(tri, gl, preferred_element_type=F32)
    gfull = jnp.where(lane < HEADS, beta, gc)
    g_ref[...] = gfull
    gt = gfull.T
    for c in range(nchunk):
        gt_ref[c] = gt[:, c * CHUNK:(c + 1) * CHUNK]

    ri = lax.broadcasted_iota(jnp.int32, (CHUNK, CHUNK), 0)
    ci = lax.broadcasted_iota(jnp.int32, (CHUNK, CHUNK), 1)
    tril = ri >= ci
    eye = jnp.where(ri == ci, 1.0, 0.0).astype(F32)
    level_masks = []
    s = 1
    while s < CHUNK:
        level_masks.append((ri // (2 * s) == ci // (2 * s)) & ((ri // s) % 2 == 1) & ((ci // s) % 2 == 0))
        s *= 2

    cpi = min(CHUNKS_PER_ITER, nchunk)

    def chunk_body(ci, carry):
        ps = [(cc, h) for cc in range(cpi) for h in range(HEADS)]
        r0 = [pl.multiple_of((ci * cpi + cc) * CHUNK, CHUNK) for cc in range(cpi)]
        gcb = [g_ref[pl.ds(r0[cc], CHUNK), :] for cc in range(cpi)]
        gtb = [gt_ref[ci * cpi + cc] for cc in range(cpi)]
        beta_c = [gcb[cc][:, h:h + 1] for cc, h in ps]
        gc_c = [gcb[cc][:, HEADS + h:HEADS + h + 1] for cc, h in ps]
        gc_r = [gtb[cc][HEADS + h:HEADS + h + 1, :] for cc, h in ps]
        q = [act_ref[h, pl.ds(r0[cc], CHUNK), :] for cc, h in ps]
        k = [act_ref[HEADS + h, pl.ds(r0[cc], CHUNK), :] for cc, h in ps]
        v = [act_ref[2 * HEADS + h, pl.ds(r0[cc], CHUNK), :] for cc, h in ps]
        n = range(len(ps))

        decay = [jnp.exp(jnp.where(tril, gc_c[i] - gc_r[i], -jnp.inf)) for i in n]
        kq = [_mm_nt(jnp.concatenate([k[i], q[i]], axis=0), k[i]) for i in n]
        m = [beta_c[i] * kq[i][:CHUNK] * decay[i] for i in n]
        attn = [kq[i][CHUNK:] * decay[i] for i in n]

        x = [eye - jnp.where(level_masks[0], m[i], 0.0) for i in n]
        for lm in level_masks[1:]:
            pm = [_mm(jnp.where(lm, m[i], 0.0), x[i]) for i in n]
            x = [x[i] - _mm(x[i], pm[i]) for i in n]

        egc = [jnp.exp(gc_c[i]) for i in n]
        rhs = [jnp.concatenate([v[i] * beta_c[i], k[i] * (beta_c[i] * egc[i])], axis=1) for i in n]
        uw = [_mm(x[i], rhs[i]) for i in n]

        g_last = [gc_c[i][CHUNK - 1:CHUNK, :] for i in n]
        k_dec = [k[i] * jnp.exp(g_last[i] - gc_c[i]) for i in n]
        u = [uw[i][:, :HEAD_DIM].astype(BF16) for i in n]
        w = [uw[i][:, HEAD_DIM:] for i in n]
        gmat = [_mm_tn(k_dec[i], w[i]) for i in n]
        aw = [_mm(attn[i], w[i]) for i in n]
        lhs = [jnp.concatenate([
            jnp.concatenate([(q[i] * egc[i] - aw[i]).astype(BF16), attn[i].astype(BF16)], axis=1),
            jnp.concatenate([(-gmat[i]).astype(BF16), k_dec[i].T.astype(BF16)], axis=1)], axis=0) for i in n]
        st = [s_ref[h] for h in range(HEADS)]
        for cc in range(cpi):
            idx = [cc * HEADS + h for h in range(HEADS)]
            r = [jnp.dot(lhs[i], jnp.concatenate([st[h].astype(BF16), u[i]], axis=0),
                         preferred_element_type=F32) for h, i in enumerate(idx)]
            for h, i in enumerate(idx):
                o_ref[pl.ds(r0[cc], CHUNK), h * HEAD_DIM:(h + 1) * HEAD_DIM] = r[h][:CHUNK]
            st = [st[h] * jnp.exp(g_last[i]) + r[h][CHUNK:] for h, i in enumerate(idx)]
        for h in range(HEADS):
            s_ref[h] = st[h]
        return carry

    lax.fori_loop(0, nchunk // cpi, chunk_body, 0)

    parts = []
    for h in range(HEADS):
        o = o_ref[:, h * HEAD_DIM:(h + 1) * HEAD_DIM]
        z = z_ref[:, h * HEAD_DIM:(h + 1) * HEAD_DIM].astype(F32)
        on = o * lax.rsqrt(jnp.mean(o * o, axis=-1, keepdims=True) + EPS) * nw_ref[...]
        parts.append((on * _silu(z)).astype(BF16))
    og = jnp.concatenate(parts, axis=1)
    out_ref[...] = jnp.dot(og, wo_ref[...], preferred_element_type=F32).astype(out_ref.dtype)


def _delta_branch(p, ba, conv_w, alog, dtb, norm_w, w_o, *, batch, seq, tt):
    width = HEADS * HEAD_DIM
    nt = seq // tt
    d_out = w_o.shape[1]
    kern = functools.partial(_delta_kernel, tt=tt)
    row = lambda b, t: b * nt + t
    return pl.pallas_call(
        kern,
        grid=(batch, nt),
        in_specs=[
            pl.BlockSpec((tt, 3 * width), lambda b, t: (row(b, t), 0)),
            pl.BlockSpec((tt, width), lambda b, t: (row(b, t), 3)),
            pl.BlockSpec((tt, LANES), lambda b, t: (row(b, t), 0)),
            pl.BlockSpec((3 * HEADS, SUBLANES, HEAD_DIM), lambda b, t: (0, 0, 0)),
            pl.BlockSpec((1, LANES), lambda b, t: (0, 0)),
            pl.BlockSpec((1, LANES), lambda b, t: (0, 0)),
            pl.BlockSpec((1, HEAD_DIM), lambda b, t: (0, 0)),
            pl.BlockSpec((width, d_out), lambda b, t: (0, 0)),
        ],
        out_specs=pl.BlockSpec((tt, d_out), lambda b, t: (row(b, t), 0)),
        out_shape=jax.ShapeDtypeStruct((batch * seq, d_out), BF16),
        scratch_shapes=[
            pltpu.VMEM((3 * HEADS, SUBLANES + tt, HEAD_DIM), F32),
            pltpu.VMEM((3 * HEADS, tt, HEAD_DIM), F32),
            pltpu.VMEM((tt, LANES), F32),
            pltpu.VMEM((tt // CHUNK, LANES, CHUNK), F32),
            pltpu.VMEM((HEADS, HEAD_DIM, HEAD_DIM), F32),
            pltpu.VMEM((tt, width), F32),
        ],
        compiler_params=pltpu.CompilerParams(
            dimension_semantics=("arbitrary", "arbitrary"),
            vmem_limit_bytes=VMEM_LIMIT_BYTES),
        name="delta_branch",
    )(p, p, ba, conv_w, alog, dtb, norm_w, w_o)


def _convmod_kernel(glu_ref, bglu_ref, dw_ref, dwb_ref, lnw_ref, lnb_ref, w2_ref, b2_ref, out_ref,
                    x0_ref, x1_ref, xf_ref, y_ref, c_ref, *, tt):
    width = lnw_ref.shape[1]
    nslab = width // LANES
    hw = 2 * SUBLANES
    rb = 2 * SUBLANES
    group = 4
    t = pl.program_id(1)

    @pl.when(t == 0)
    def _():
        x0_ref[:, 0:hw, :] = jnp.zeros((nslab, hw, LANES), jnp.uint32)
        x1_ref[:, 0:hw, :] = jnp.zeros((nslab, hw, LANES), jnp.uint32)
        xf_ref[:, 0:SUBLANES, :] = jnp.zeros((nslab, SUBLANES, LANES), F32)

    for j in range(nslab):
        ca = slice(j * LANES, (j + 1) * LANES)
        cb = slice(width + j * LANES, width + (j + 1) * LANES)
        ga = glu_ref[:, ca].astype(F32) + bglu_ref[:, ca]
        gb = glu_ref[:, cb].astype(F32) + bglu_ref[:, cb]
        c = ga * _sigmoid(gb)
        xf_ref[j, SUBLANES:SUBLANES + tt, :] = c
        x0_ref[j, hw:hw + tt // 2, :] = pltpu.bitcast(c.astype(BF16), jnp.uint32)
        shifted = xf_ref[j, SUBLANES - 1:SUBLANES - 1 + tt, :]
        x1_ref[j, hw:hw + tt // 2, :] = pltpu.bitcast(shifted.astype(BF16), jnp.uint32)

    def conv_body(j, carry):
        for g0 in range(0, tt, group * rb):
            acc = [None] * group
            for k in range(CM_KERNEL):
                wk = dw_ref[j, k].astype(F32)
                for b in range(group):
                    off = g0 + b * rb - (CM_KERNEL - 1) + k
                    if off % 2 == 0:
                        words = x0_ref[j, hw + off // 2:hw + off // 2 + rb // 2, :]
                    else:
                        words = x1_ref[j, hw + (off + 1) // 2:hw + (off + 1) // 2 + rb // 2, :]
                    term = pltpu.bitcast(words, BF16).astype(F32) * wk
                    acc[b] = term if acc[b] is None else acc[b] + term
            for b in range(group):
                r0 = g0 + b * rb
                y_ref[j, r0:r0 + rb, :] = acc[b] + dwb_ref[j]
        return carry

    lax.fori_loop(0, nslab, conv_body, 0)
    x0_ref[:, 0:hw, :] = x0_ref[:, tt // 2:tt // 2 + hw, :]
    x1_ref[:, 0:hw, :] = x1_ref[:, tt // 2:tt // 2 + hw, :]
    xf_ref[:, 0:SUBLANES, :] = xf_ref[:, tt:tt + SUBLANES, :]

    tot = y_ref[0]
    for j in range(1, nslab):
        tot = tot + y_ref[j]
    mu = jnp.sum(tot, axis=-1, keepdims=True) * (1.0 / width)
    sq = None
    for j in range(nslab):
        cc = y_ref[j] - mu
        sq = cc * cc if sq is None else sq + cc * cc
    rstd = lax.rsqrt(jnp.sum(sq, axis=-1, keepdims=True) * (1.0 / width) + EPS)
    for j in range(nslab):
        ca = slice(j * LANES, (j + 1) * LANES)
        y = (y_ref[j] - mu) * rstd * lnw_ref[:, ca] + lnb_ref[:, ca]
        c_ref[:, ca] = _silu(y).astype(BF16)

    out = jnp.dot(c_ref[...], w2_ref[...], preferred_element_type=F32) + b2_ref[...]
    out_ref[...] = out.astype(out_ref.dtype)


def _conv_branch(p, b_glu, dw_taps, dw_bias, ln_w, ln_b, w_pw2, b_pw2, *, batch, seq, tt):
    width = ln_w.shape[1]
    nslab = width // LANES
    d_out = w_pw2.shape[1]
    nt = seq // tt
    kern = functools.partial(_convmod_kernel, tt=tt)
    row = lambda b, t: b * nt + t
    const = lambda b, t: (0, 0)
    return pl.pallas_call(
        kern,
        grid=(batch, nt),
        in_specs=[
            pl.BlockSpec((tt, 2 * width), lambda b, t: (row(b, t), 2)),
            pl.BlockSpec((1, 2 * width), const),
            pl.BlockSpec((nslab, CM_KERNEL, 2 * SUBLANES, LANES), lambda b, t: (0, 0, 0, 0)),
            pl.BlockSpec((nslab, 1, LANES), lambda b, t: (0, 0, 0)),
            pl.BlockSpec((1, width), const),
            pl.BlockSpec((1, width), const),
            pl.BlockSpec((width, d_out), const),
            pl.BlockSpec((1, d_out), const),
        ],
        out_specs=pl.BlockSpec((tt, d_out), lambda b, t: (row(b, t), 0)),
        out_shape=jax.ShapeDtypeStruct((batch * seq, d_out), BF16),
        scratch_shapes=[
            pltpu.VMEM((nslab, 2 * SUBLANES + tt // 2, LANES), jnp.uint32),
            pltpu.VMEM((nslab, 2 * SUBLANES + tt // 2, LANES), jnp.uint32),
            pltpu.VMEM((nslab, SUBLANES + tt, LANES), F32),
            pltpu.VMEM((nslab, tt, LANES), F32),
            pltpu.VMEM((tt, width), BF16),
        ],
        compiler_params=pltpu.CompilerParams(
            dimension_semantics=("arbitrary", "arbitrary"),
            vmem_limit_bytes=VMEM_LIMIT_BYTES),
        name="conv_branch",
    )(p, b_glu, dw_taps, dw_bias, ln_w, ln_b, w_pw2, b_pw2)


def _merge_ffn_kernel(x_ref, oa_ref, ob_ref, gates_ref, bg_ref, wout_ref, n2_ref, wgu_ref, wdn_ref,
                      nf_ref, out_ref):
    d = x_ref.shape[1]
    hidden = wdn_ref.shape[0]
    tm = x_ref.shape[0]
    nsub = SUBTILES if tm % (SUBTILES * 2 * SUBLANES) == 0 else 1
    rows = [slice(i * (tm // nsub), (i + 1) * (tm // nsub)) for i in range(nsub)]
    merged = []
    for r in rows:
        gate_a = _sigmoid(gates_ref[r, :d].astype(F32) + bg_ref[:, :d])
        gate_b = _sigmoid(gates_ref[r, d:].astype(F32) + bg_ref[:, d:])
        merged.append((gate_a * oa_ref[r, :].astype(F32) + gate_b * ob_ref[r, :].astype(F32)).astype(BF16))
    x1 = [x_ref[r, :] + jnp.dot(m, wout_ref[...], preferred_element_type=F32) for r, m in zip(rows, merged)]
    h2 = [(v * lax.rsqrt(jnp.mean(v * v, axis=-1, keepdims=True) + EPS) * n2_ref[...]).astype(BF16) for v in x1]
    gu = [jnp.dot(h, wgu_ref[...], preferred_element_type=F32) for h in h2]
    act = [(_silu(g[:, :hidden]) * g[:, hidden:]).astype(BF16) for g in gu]
    x2 = [v + jnp.dot(a, wdn_ref[...], preferred_element_type=F32) for v, a in zip(x1, act)]
    for r, v in zip(rows, x2):
        out_ref[r, :] = v * lax.rsqrt(jnp.mean(v * v, axis=-1, keepdims=True) + EPS) * nf_ref[...]


def _merge_ffn(x2d, out_a, out_b, p, b_gates, w_out, norm2_w, w_gu, w_dn, norm_f_w, *, tm):
    n, d = x2d.shape
    hidden = w_dn.shape[0]
    const = lambda i: (0, 0)
    single = dict(pipeline_mode=pl.Buffered(1))
    return pl.pallas_call(
        _merge_ffn_kernel,
        grid=(n // tm,),
        in_specs=[
            pl.BlockSpec((tm, d), lambda i: (i, 0)),
            pl.BlockSpec((tm, d), lambda i: (i, 0)),
            pl.BlockSpec((tm, d), lambda i: (i, 0)),
            pl.BlockSpec((tm, 2 * d), lambda i: (i, 3)),
            pl.BlockSpec((1, 2 * d), const),
            pl.BlockSpec((d, d), const, **single),
            pl.BlockSpec((1, d), const),
            pl.BlockSpec((d, 2 * hidden), const, **single),
            pl.BlockSpec((hidden, d), const, **single),
            pl.BlockSpec((1, d), const),
        ],
        out_specs=pl.BlockSpec((tm, d), lambda i: (i, 0)),
        out_shape=jax.ShapeDtypeStruct((n, d), F32),
        compiler_params=pltpu.CompilerParams(
            dimension_semantics=("arbitrary",),
            vmem_limit_bytes=VMEM_LIMIT_BYTES),
        name="merge_ffn",
    )(x2d, out_a, out_b, p, b_gates, w_out, norm2_w, w_gu, w_dn, norm_f_w)


def _pick_tile(n, target):
    t = min(n, target)
    while n % t:
        t //= 2
    return t


def kernel(x, norm1_w, w_in, b_glu, b_gates, dn_conv_w, dn_A_log, dn_dt_bias, dn_norm_w, dn_w_o,
           cm_dw_w, cm_dw_b, cm_ln_w, cm_ln_b, cm_w_pw2, cm_b_pw2, w_out, norm2_w, ffn_w_gate_up,
           ffn_w_down, norm_f_w):
    batch, seq, d = x.shape
    depth = w_in.shape[0]
    width = HEADS * HEAD_DIM
    assert d == width and cm_ln_w.shape[1] == d and seq % CHUNK == 0
    n = batch * seq
    tt = _pick_tile(seq, 512)
    xf = x.reshape(n, d)

    for l in range(depth):
        wl = w_in[l]
        w_main = jnp.concatenate([wl[:, :4 * width], wl[:, 4 * width + 2 * HEADS:]], axis=1).astype(BF16)
        w_ba = jnp.pad(wl[:, 4 * width:4 * width + 2 * HEADS], ((0, 0), (0, LANES - 2 * HEADS)))
        w_ba_hi = w_ba.astype(BF16)
        w_ba_lo = (w_ba - w_ba_hi.astype(F32)).astype(BF16)
        w_ba2 = jnp.concatenate([w_ba_hi, w_ba_lo], axis=1)
        lane_pad = lambda v: jnp.pad(v.reshape(1, HEADS), ((0, 0), (HEADS, LANES - 2 * HEADS)))

        p, ba = _inproj(xf, norm1_w[l].reshape(1, d), w_main, w_ba2,
                        tm=_pick_tile(n, 512), tn=w_main.shape[1])
        dn_cw = jnp.pad(dn_conv_w[l].reshape(DN_CONV, 3 * HEADS, HEAD_DIM).transpose(1, 0, 2),
                        ((0, 0), (0, SUBLANES - DN_CONV), (0, 0)))
        cm_taps = cm_dw_w[l].reshape(CM_KERNEL, d // LANES, LANES).transpose(1, 0, 2).astype(BF16)
        cm_taps = jnp.broadcast_to(cm_taps[:, :, None, :], (d // LANES, CM_KERNEL, 2 * SUBLANES, LANES))
        cm_bias = cm_dw_b[l].reshape(d // LANES, 1, LANES)

        out_a = _delta_branch(p, ba, dn_cw, lane_pad(dn_A_log[l]), lane_pad(dn_dt_bias[l]),
                              dn_norm_w[l].reshape(1, HEAD_DIM), dn_w_o[l].astype(BF16),
                              batch=batch, seq=seq, tt=tt)
        out_b = _conv_branch(p, b_glu[l].reshape(1, 2 * d), cm_taps, cm_bias,
                             cm_ln_w[l].reshape(1, d), cm_ln_b[l].reshape(1, d),
                             cm_w_pw2[l].astype(BF16), cm_b_pw2[l].reshape(1, d),
                             batch=batch, seq=seq, tt=tt)
        assert depth == 1
        xf = _merge_ffn(xf, out_a, out_b, p, b_gates[l].reshape(1, 2 * d), w_out[l].astype(BF16),
                        norm2_w[l].reshape(1, d), ffn_w_gate_up[l].astype(BF16),
                        ffn_w_down[l].astype(BF16), norm_f_w.reshape(1, d), tm=_pick_tile(n, 512))
    return xf.reshape(batch, seq, d)
```

```python
import functools

import jax
import jax.numpy as jnp
from jax import lax
from jax.experimental import pallas as pl
from jax.experimental.pallas import tpu as pltpu

F32 = jnp.float32
BF16 = jnp.bfloat16

EPS = 1e-6
CHUNK = 64
HEADS = 8
HEAD_DIM = 128
DN_CONV = 4
CM_KERNEL = 31
LANES = 128
SUBLANES = 8
VMEM_LIMIT_BYTES = 56 * 1024 * 1024
SUBTILES = 2
CHUNKS_PER_ITER = 4


def _mm(a, b):
    return jnp.dot(a.astype(BF16), b.astype(BF16), preferred_element_type=F32)


def _mm_nt(a, b):
    return lax.dot_general(a.astype(BF16), b.astype(BF16), (((1,), (1,)), ((), ())),
                           preferred_element_type=F32)


def _mm_tn(a, b):
    return lax.dot_general(a.astype(BF16), b.astype(BF16), (((0,), (0,)), ((), ())),
                           preferred_element_type=F32)


def _split(a):
    hi = a.astype(BF16)
    lo = (a - hi.astype(F32)).astype(BF16)
    return hi, lo


def _sigmoid(x):
    return 1.0 / (1.0 + jnp.exp(-x))


def _silu(x):
    return x * _sigmoid(x)


def _inproj_kernel(x_ref, nw_ref, w_ref, wba_ref, p_ref, ba_ref):
    tm = x_ref.shape[0]
    nsub = SUBTILES if tm % (SUBTILES * 2 * SUBLANES) == 0 else 1
    rows = [slice(i * (tm // nsub), (i + 1) * (tm // nsub)) for i in range(nsub)]
    hs = []
    for r in rows:
        x = x_ref[r, :]
        ms = jnp.mean(x * x, axis=-1, keepdims=True)
        hs.append(_split(x * lax.rsqrt(ms + EPS) * nw_ref[...]))
    for r, (hh, hl) in zip(rows, hs):
        p_ref[r, :] = jnp.dot(hh, w_ref[...], preferred_element_type=F32).astype(p_ref.dtype)
        t = jnp.dot(hh, wba_ref[...], preferred_element_type=F32)
        t2 = jnp.dot(hl, wba_ref[:, :LANES], preferred_element_type=F32)
        ba_ref[r, :] = t[:, :LANES] + (t[:, LANES:] + t2)


def _inproj(x2, norm_w, w_main, w_ba, *, tm):
    n, d = x2.shape
    cols = w_main.shape[1]
    return pl.pallas_call(
        _inproj_kernel,
        grid=(n // tm,),
        in_specs=[
            pl.BlockSpec((tm, d), lambda i: (i, 0)),
            pl.BlockSpec((1, d), lambda i: (0, 0)),
            pl.BlockSpec((d, cols), lambda i: (0, 0), pipeline_mode=pl.Buffered(1)),
            pl.BlockSpec((d, 2 * LANES), lambda i: (0, 0)),
        ],
        out_specs=[
            pl.BlockSpec((tm, cols), lambda i: (i, 0)),
            pl.BlockSpec((tm, LANES), lambda i: (i, 0)),
        ],
        out_shape=[
            jax.ShapeDtypeStruct((n, cols), BF16),
            jax.ShapeDtypeStruct((n, LANES), F32),
        ],
        compiler_params=pltpu.CompilerParams(
            dimension_semantics=("arbitrary",),
            vmem_limit_bytes=VMEM_LIMIT_BYTES),
        name="inproj",
    )(x2, norm_w, w_main, w_ba)


def _delta_kernel(qkv_ref, z_ref, ba_ref, cw_ref, alog_ref, dtb_ref, nw_ref, wo_ref, out_ref,
                  xh_ref, act_ref, g_ref, gt_ref, s_ref, o_ref, *, tt):
    width = HEADS * HEAD_DIM
    nchunk = tt // CHUNK
    nslab = 3 * HEADS
    hist = SUBLANES
    t = pl.program_id(1)

    @pl.when(t == 0)
    def _():
        xh_ref[:, 0:hist, :] = jnp.zeros((nslab, hist, HEAD_DIM), F32)
        s_ref[...] = jnp.zeros_like(s_ref)

    for j in range(nslab):
        xh_ref[j, hist:hist + tt, :] = qkv_ref[:, j * HEAD_DIM:(j + 1) * HEAD_DIM].astype(F32)

    rb = 4 * SUBLANES

    def conv_slab(j, post):
        for r0 in range(0, tt, rb):
            acc = None
            for k in range(DN_CONV):
                lo = r0 + hist - (DN_CONV - 1) + k
                term = xh_ref[j, lo:lo + rb, :] * cw_ref[j, k:k + 1, :]
                acc = term if acc is None else acc + term
            y = _silu(acc)
            if post is not None:
                y = y * (lax.rsqrt(jnp.sum(y * y, axis=-1, keepdims=True) + EPS) * post)
            act_ref[j, r0:r0 + rb, :] = y

    def qk_body(j, carry):
        conv_slab(j, jnp.where(j < HEADS, HEAD_DIM ** -0.5, 1.0).astype(F32))
        return carry

    def v_body(j, carry):
        conv_slab(j, None)
        return carry

    lax.fori_loop(0, 2 * HEADS, qk_body, 0, unroll=4)
    lax.fori_loop(2 * HEADS, nslab, v_body, 0, unroll=4)
    xh_ref[:, 0:hist, :] = xh_ref[:, tt:tt + hist, :]

    ba = ba_ref[...]
    lane = lax.broadcasted_iota(jnp.int32, (tt, LANES), 1)
    beta = _sigmoid(ba)
    sp_in = ba + dtb_ref[...]
    softplus = jnp.maximum(sp_in, 0.0) + jnp.log1p(jnp.exp(-jnp.abs(sp_in)))
    g = -jnp.exp(alog_ref[...]) * softplus
    g = jnp.where((lane >= HEADS) & (lane < 2 * HEADS), g, 0.0)
    row = lax.broadcasted_iota(jnp.int32, (tt, tt), 0)
    col = lax.broadcasted_iota(jnp.int32, (tt, tt), 1)
    tri = jnp.where((row // CHUNK == col // CHUNK) & (row >= col), 1.0, 0.0).astype(BF16)
    gh, gl = _split(g)
    gc = jnp.dot(tri, gh, preferred_element_type=F32) + jnp.dot(tri, gl, preferred_element_type=F32)
    gfull = jnp.where(lane < HEADS, beta, gc)
    g_ref[...] = gfull
    gt = gfull.T
    for c in range(nchunk):
        gt_ref[c] = gt[:, c * CHUNK:(c + 1) * CHUNK]

    ri = lax.broadcasted_iota(jnp.int32, (CHUNK, CHUNK), 0)
    ci = lax.broadcasted_iota(jnp.int32, (CHUNK, CHUNK), 1)
    tril = ri >= ci
    eye = jnp.where(ri == ci, 1.0, 0.0).astype(F32)
    level_masks = []
    s = 1
    while s < CHUNK:
        level_masks.append((ri // (2 * s) == ci // (2 * s)) & ((ri // s) % 2 == 1) & ((ci // s) % 2 == 0))
        s *= 2

    cpi = min(CHUNKS_PER_ITER, nchunk)

    def chunk_body(ci, carry):
        ps = [(cc, h) for cc in range(cpi) for h in range(HEADS)]
        r0 = [pl.multiple_of((ci * cpi + cc) * CHUNK, CHUNK) for cc in range(cpi)]
        gcb = [g_ref[pl.ds(r0[cc], CHUNK), :] for cc in range(cpi)]
        gtb = [gt_ref[ci * cpi + cc] for cc in range(cpi)]
        beta_c = [gcb[cc][:, h:h + 1] for cc, h in ps]
        gc_c = [gcb[cc][:, HEADS + h:HEADS + h + 1] for cc, h in ps]
        gc_r = [gtb[cc][HEADS + h:HEADS + h + 1, :] for cc, h in ps]
        q = [act_ref[h, pl.ds(r0[cc], CHUNK), :] for cc, h in ps]
        k = [act_ref[HEADS + h, pl.ds(r0[cc], CHUNK), :] for cc, h in ps]
        v = [act_ref[2 * HEADS + h, pl.ds(r0[cc], CHUNK), :] for cc, h in ps]
        n = range(len(ps))

        decay = [jnp.exp(jnp.where(tril, gc_c[i] - gc_r[i], -jnp.inf)) for i in n]
        kq = [_mm_nt(jnp.concatenate([k[i], q[i]], axis=0), k[i]) for i in n]
        m = [beta_c[i] * kq[i][:CHUNK] * decay[i] for i in n]
        attn = [kq[i][CHUNK:] * decay[i] for i in n]

        x = [eye - jnp.where(level_masks[0], m[i], 0.0) for i in n]
        for lm in level_masks[1:]:
            pm = [_mm(jnp.where(lm, m[i], 0.0), x[i]) for i in n]
            x = [x[i] - _mm(x[i], pm[i]) for i in n]

        egc = [jnp.exp(gc_c[i]) for i in n]
        rhs = [jnp.concatenate([v[i] * beta_c[i], k[i] * (beta_c[i] * egc[i])], axis=1) for i in n]
        uw = [_mm(x[i], rhs[i]) for i in n]

        g_last = [gc_c[i][CHUNK - 1:CHUNK, :] for i in n]
        k_dec = [k[i] * jnp.exp(g_last[i] - gc_c[i]) for i in n]
        u = [uw[i][:, :HEAD_DIM].astype(BF16) for i in n]
        w = [uw[i][:, HEAD_DIM:] for i in n]
        gmat = [_mm_tn(k_dec[i], w[i]) for i in n]
        aw = [_mm(attn[i], w[i]) for i in n]
        lhs = [jnp.concatenate([
            jnp.concatenate([(q[i] * egc[i] - aw[i]).astype(BF16), attn[i].astype(BF16)], axis=1),
            jnp.concatenate([(-gmat[i]).astype(BF16), k_dec[i].T.astype(BF16)], axis=1)], axis=0) for i in n]
        st = [s_ref[h] for h in range(HEADS)]
        for cc in range(cpi):
            idx = [cc * HEADS + h for h in range(HEADS)]
            r = [jnp.dot(lhs[i], jnp.concatenate([st[h].astype(BF16), u[i]], axis=0),
                         preferred_element_type=F32) for h, i in enumerate(idx)]
            for h, i in enumerate(idx):
                o_ref[pl.ds(r0[cc], CHUNK), h * HEAD_DIM:(h + 1) * HEAD_DIM] = r[h][:CHUNK]
            st = [st[h] * jnp.exp(g_last[i]) + r[h][CHUNK:] for h, i in enumerate(idx)]
        for h in range(HEADS):
            s_ref[h] = st[h]
        return carry

    lax.fori_loop(0, nchunk // cpi, chunk_body, 0)

    parts = []
    for h in range(HEADS):
        o = o_ref[:, h * HEAD_DIM:(h + 1) * HEAD_DIM]
        z = z_ref[:, h * HEAD_DIM:(h + 1) * HEAD_DIM].astype(F32)
        on = o * lax.rsqrt(jnp.mean(o * o, axis=-1, keepdims=True) + EPS) * nw_ref[...]
        parts.append((on * _silu(z)).astype(BF16))
    og = jnp.concatenate(parts, axis=1)
    out_ref[...] = jnp.dot(og, wo_ref[...], preferred_element_type=F32).astype(out_ref.dtype)


def _delta_branch(p, ba, conv_w, alog, dtb, norm_w, w_o, *, batch, seq, tt):
    width = HEADS * HEAD_DIM
    nt = seq // tt
    d_out = w_o.shape[1]
    kern = functools.partial(_delta_kernel, tt=tt)
    row = lambda b, t: b * nt + t
    return pl.pallas_call(
        kern,
        grid=(batch, nt),
        in_specs=[
            pl.BlockSpec((tt, 3 * width), lambda b, t: (row(b, t), 0)),
            pl.BlockSpec((tt, width), lambda b, t: (row(b, t), 3)),
            pl.BlockSpec((tt, LANES), lambda b, t: (row(b, t), 0)),
            pl.BlockSpec((3 * HEADS, SUBLANES, HEAD_DIM), lambda b, t: (0, 0, 0)),
            pl.BlockSpec((1, LANES), lambda b, t: (0, 0)),
            pl.BlockSpec((1, LANES), lambda b, t: (0, 0)),
            pl.BlockSpec((1, HEAD_DIM), lambda b, t: (0, 0)),
            pl.BlockSpec((width, d_out), lambda b, t: (0, 0)),
        ],
        out_specs=pl.BlockSpec((tt, d_out), lambda b, t: (row(b, t), 0)),
        out_shape=jax.ShapeDtypeStruct((batch * seq, d_out), BF16),
        scratch_shapes=[
            pltpu.VMEM((3 * HEADS, SUBLANES + tt, HEAD_DIM), F32),
            pltpu.VMEM((3 * HEADS, tt, HEAD_DIM), F32),
            pltpu.VMEM((tt, LANES), F32),
            pltpu.VMEM((tt // CHUNK, LANES, CHUNK), F32),
            pltpu.VMEM((HEADS, HEAD_DIM, HEAD_DIM), F32),
            pltpu.VMEM((tt, width), F32),
        ],
        compiler_params=pltpu.CompilerParams(
            dimension_semantics=("arbitrary", "arbitrary"),
            vmem_limit_bytes=VMEM_LIMIT_BYTES),
        name="delta_branch",
    )(p, p, ba, conv_w, alog, dtb, norm_w, w_o)


def _convmod_kernel(glu_ref, bglu_ref, dw_ref, dwb_ref, lnw_ref, lnb_ref, w2_ref, b2_ref, out_ref,
                    x0_ref, x1_ref, xf_ref, y_ref, c_ref, *, tt):
    width = lnw_ref.shape[1]
    nslab = width // LANES
    hw = 2 * SUBLANES
    rb = 2 * SUBLANES
    group = 4
    t = pl.program_id(1)

    @pl.when(t == 0)
    def _():
        x0_ref[:, 0:hw, :] = jnp.zeros((nslab, hw, LANES), jnp.uint32)
        x1_ref[:, 0:hw, :] = jnp.zeros((nslab, hw, LANES), jnp.uint32)
        xf_ref[:, 0:SUBLANES, :] = jnp.zeros((nslab, SUBLANES, LANES), F32)

    for j in range(nslab):
        ca = slice(j * LANES, (j + 1) * LANES)
        cb = slice(width + j * LANES, width + (j + 1) * LANES)
        ga = glu_ref[:, ca].astype(F32) + bglu_ref[:, ca]
        gb = glu_ref[:, cb].astype(F32) + bglu_ref[:, cb]
        c = ga * _sigmoid(gb)
        xf_ref[j, SUBLANES:SUBLANES + tt, :] = c
        x0_ref[j, hw:hw + tt // 2, :] = pltpu.bitcast(c.astype(BF16), jnp.uint32)
        shifted = xf_ref[j, SUBLANES - 1:SUBLANES - 1 + tt, :]
        x1_ref[j, hw:hw + tt // 2, :] = pltpu.bitcast(shifted.astype(BF16), jnp.uint32)

    def conv_body(j, carry):
        for g0 in range(0, tt, group * rb):
            acc = [None] * group
            for k in range(CM_KERNEL):
                wk = dw_ref[j, k].astype(F32)
                for b in range(group):
                    off = g0 + b * rb - (CM_KERNEL - 1) + k
                    if off % 2 == 0:
                        words = x0_ref[j, hw + off // 2:hw + off // 2 + rb // 2, :]
                    else:
                        words = x1_ref[j, hw + (off + 1) // 2:hw + (off + 1) // 2 + rb // 2, :]
                    term = pltpu.bitcast(words, BF16).astype(F32) * wk
                    acc[b] = term if acc[b] is None else acc[b] + term
            for b in range(group):
                r0 = g0 + b * rb
                y_ref[j, r0:r0 + rb, :] = acc[b] + dwb_ref[j]
        return carry

    lax.fori_loop(0, nslab, conv_body, 0)
    x0_ref[:, 0:hw, :] = x0_ref[:, tt // 2:tt // 2 + hw, :]
    x1_ref[:, 0:hw, :] = x1_ref[:, tt // 2:tt // 2 + hw, :]
    xf_ref[:, 0:SUBLANES, :] = xf_ref[:, tt:tt + SUBLANES, :]

    tot = y_ref[0]
    for j in range(1, nslab):
        tot = tot + y_ref[j]
    mu = jnp.sum(tot, axis=-1, keepdims=True) * (1.0 / width)
    sq = None
    for j in range(nslab):
        cc = y_ref[j] - mu
        sq = cc * cc if sq is None else sq + cc * cc
    rstd = lax.rsqrt(jnp.sum(sq, axis=-1, keepdims=True) * (1.0 / width) + EPS)
    for j in range(nslab):
        ca = slice(j * LANES, (j + 1) * LANES)
        y = (y_ref[j] - mu) * rstd * lnw_ref[:, ca] + lnb_ref[:, ca]
        c_ref[:, ca] = _silu(y).astype(BF16)

    out = jnp.dot(c_ref[...], w2_ref[...], preferred_element_type=F32) + b2_ref[...]
    out_ref[...] = out.astype(out_ref.dtype)


def _conv_branch(p, b_glu, dw_taps, dw_bias, ln_w, ln_b, w_pw2, b_pw2, *, batch, seq, tt):
    width = ln_w.shape[1]
    nslab = width // LANES
    d_out = w_pw2.shape[1]
    nt = seq // tt
    kern = functools.partial(_convmod_kernel, tt=tt)
    row = lambda b, t: b * nt + t
    const = lambda b, t: (0, 0)
    return pl.pallas_call(
        kern,
        grid=(batch, nt),
        in_specs=[
            pl.BlockSpec((tt, 2 * width), lambda b, t: (row(b, t), 2)),
            pl.BlockSpec((1, 2 * width), const),
            pl.BlockSpec((nslab, CM_KERNEL, 2 * SUBLANES, LANES), lambda b, t: (0, 0, 0, 0)),
            pl.BlockSpec((nslab, 1, LANES), lambda b, t: (0, 0, 0)),
            pl.BlockSpec((1, width), const),
            pl.BlockSpec((1, width), const),
            pl.BlockSpec((width, d_out), const),
            pl.BlockSpec((1, d_out), const),
        ],
        out_specs=pl.BlockSpec((tt, d_out), lambda b, t: (row(b, t), 0)),
        out_shape=jax.ShapeDtypeStruct((batch * seq, d_out), BF16),
        scratch_shapes=[
            pltpu.VMEM((nslab, 2 * SUBLANES + tt // 2, LANES), jnp.uint32),
            pltpu.VMEM((nslab, 2 * SUBLANES + tt // 2, LANES), jnp.uint32),
            pltpu.VMEM((nslab, SUBLANES + tt, LANES), F32),
            pltpu.VMEM((nslab, tt, LANES), F32),
            pltpu.VMEM((tt, width), BF16),
        ],
        compiler_params=pltpu.CompilerParams(
            dimension_semantics=("arbitrary", "arbitrary"),
            vmem_limit_bytes=VMEM_LIMIT_BYTES),
        name="conv_branch",
    )(p, b_glu, dw_taps, dw_bias, ln_w, ln_b, w_pw2, b_pw2)


def _merge_ffn_kernel(x_ref, oa_ref, ob_ref, gates_ref, bg_ref, wout_ref, n2_ref, wgu_ref, wdn_ref,
                      nf_ref, out_ref):
    d = x_ref.shape[1]
    hidden = wdn_ref.shape[0]
    tm = x_ref.shape[0]
    nsub = SUBTILES if tm % (SUBTILES * 2 * SUBLANES) == 0 else 1
    rows = [slice(i * (tm // nsub), (i + 1) * (tm // nsub)) for i in range(nsub)]
    merged = []
    for r in rows:
        gate_a = _sigmoid(gates_ref[r, :d].astype(F32) + bg_ref[:, :d])
        gate_b = _sigmoid(gates_ref[r, d:].astype(F32) + bg_ref[:, d:])
        merged.append((gate_a * oa_ref[r, :].astype(F32) + gate_b * ob_ref[r, :].astype(F32)).astype(BF16))
    x1 = [x_ref[r, :] + jnp.dot(m, wout_ref[...], preferred_element_type=F32) for r, m in zip(rows, merged)]
    h2 = [(v * lax.rsqrt(jnp.mean(v * v, axis=-1, keepdims=True) + EPS) * n2_ref[...]).astype(BF16) for v in x1]
    gu = [jnp.dot(h, wgu_ref[...], preferred_element_type=F32) for h in h2]
    act = [(_silu(g[:, :hidden]) * g[:, hidden:]).astype(BF16) for g in gu]
    x2 = [v + jnp.dot(a, wdn_ref[...], preferred_element_type=F32) for v, a in zip(x1, act)]
    for r, v in zip(rows, x2):
        out_ref[r, :] = v * lax.rsqrt(jnp.mean(v * v, axis=-1, keepdims=True) + EPS) * nf_ref[...]


def _merge_ffn(x2d, out_a, out_b, p, b_gates, w_out, norm2_w, w_gu, w_dn, norm_f_w, *, tm):
    n, d = x2d.shape
    hidden = w_dn.shape[0]
    const = lambda i: (0, 0)
    single = dict(pipeline_mode=pl.Buffered(1))
    return pl.pallas_call(
        _merge_ffn_kernel,
        grid=(n // tm,),
        in_specs=[
            pl.BlockSpec((tm, d), lambda i: (i, 0)),
            pl.BlockSpec((tm, d), lambda i: (i, 0)),
            pl.BlockSpec((tm, d), lambda i: (i, 0)),
            pl.BlockSpec((tm, 2 * d), lambda i: (i, 3)),
            pl.BlockSpec((1, 2 * d), const),
            pl.BlockSpec((d, d), const, **single),
            pl.BlockSpec((1, d), const),
            pl.BlockSpec((d, 2 * hidden), const, **single),
            pl.BlockSpec((hidden, d), const, **single),
            pl.BlockSpec((1, d), const),
        ],
        out_specs=pl.BlockSpec((tm, d), lambda i: (i, 0)),
        out_shape=jax.ShapeDtypeStruct((n, d), F32),
        compiler_params=pltpu.CompilerParams(
            dimension_semantics=("arbitrary",),
            vmem_limit_bytes=VMEM_LIMIT_BYTES),
        name="merge_ffn",
    )(x2d, out_a, out_b, p, b_gates, w_out, norm2_w, w_gu, w_dn, norm_f_w)


def _pick_tile(n, target):
    t = min(n, target)
    while n % t:
        t //= 2
    return t


def kernel(x, norm1_w, w_in, b_glu, b_gates, dn_conv_w, dn_A_log, dn_dt_bias, dn_norm_w, dn_w_o,
           cm_dw_w, cm_dw_b, cm_ln_w, cm_ln_b, cm_w_pw2, cm_b_pw2, w_out, norm2_w, ffn_w_gate_up,
           ffn_w_down, norm_f_w):
    batch, seq, d = x.shape
    depth = w_in.shape[0]
    width = HEADS * HEAD_DIM
    assert d == width and cm_ln_w.shape[1] == d and seq % CHUNK == 0
    n = batch * seq
    tt = _pick_tile(seq, 512)
    xf = x.reshape(n, d)

    for l in range(depth):
        wl = w_in[l]
        w_main = jnp.concatenate([wl[:, :4 * width], wl[:, 4 * width + 2 * HEADS:]], axis=1).astype(BF16)
        w_ba = jnp.pad(wl[:, 4 * width:4 * width + 2 * HEADS], ((0, 0), (0, LANES - 2 * HEADS)))
        w_ba_hi = w_ba.astype(BF16)
        w_ba_lo = (w_ba - w_ba_hi.astype(F32)).astype(BF16)
        w_ba2 = jnp.concatenate([w_ba_hi, w_ba_lo], axis=1)
        lane_pad = lambda v: jnp.pad(v.reshape(1, HEADS), ((0, 0), (HEADS, LANES - 2 * HEADS)))

        p, ba = _inproj(xf, norm1_w[l].reshape(1, d), w_main, w_ba2,
                        tm=_pick_tile(n, 512))
        dn_cw = jnp.pad(dn_conv_w[l].reshape(DN_CONV, 3 * HEADS, HEAD_DIM).transpose(1, 0, 2),
                        ((0, 0), (0, SUBLANES - DN_CONV), (0, 0)))
        cm_taps = cm_dw_w[l].reshape(CM_KERNEL, d // LANES, LANES).transpose(1, 0, 2).astype(BF16)
        cm_taps = jnp.broadcast_to(cm_taps[:, :, None, :], (d // LANES, CM_KERNEL, 2 * SUBLANES, LANES))
        cm_bias = cm_dw_b[l].reshape(d // LANES, 1, LANES)

        out_a = _delta_branch(p, ba, dn_cw, lane_pad(dn_A_log[l]), lane_pad(dn_dt_bias[l]),
                              dn_norm_w[l].reshape(1, HEAD_DIM), dn_w_o[l].astype(BF16),
                              batch=batch, seq=seq, tt=tt)
        out_b = _conv_branch(p, b_glu[l].reshape(1, 2 * d), cm_taps, cm_bias,
                             cm_ln_w[l].reshape(1, d), cm_ln_b[l].reshape(1, d),
                             cm_w_pw2[l].astype(BF16), cm_b_pw2[l].reshape(1, d),
                             batch=batch, seq=seq, tt=tt)
        assert depth == 1
        xf = _merge_ffn(xf, out_a, out_b, p, b_gates[l].reshape(1, 2 * d), w_out[l].astype(BF16),
                        norm2_w[l].reshape(1, d), ffn_w_gate_up[l].astype(BF16),
                        ffn_w_down[l].astype(BF16), norm_f_w.reshape(1, d), tm=_pick_tile(n, 512))
    return xf.reshape(batch, seq, d)
```

```python
import functools

import jax
import jax.numpy as jnp
from jax import lax
from jax.experimental import pallas as pl
from jax.experimental.pallas import tpu as pltpu

F32 = jnp.float32
BF16 = jnp.bfloat16

EPS = 1e-6
CHUNK = 64
HEADS = 8
HEAD_DIM = 128
DN_CONV = 4
CM_KERNEL = 31
LANES = 128
SUBLANES = 8
VMEM_LIMIT_BYTES = 56 * 1024 * 1024
SUBTILES = 2
CHUNKS_PER_ITER = 4


def _mm(a, b):
    return jnp.dot(a.astype(BF16), b.astype(BF16), preferred_element_type=F32)


def _mm_nt(a, b):
    return lax.dot_general(a.astype(BF16), b.astype(BF16), (((1,), (1,)), ((), ())),
                           preferred_element_type=F32)


def _mm_tn(a, b):
    return lax.dot_general(a.astype(BF16), b.astype(BF16), (((0,), (0,)), ((), ())),
                           preferred_element_type=F32)


def _split(a):
    hi = a.astype(BF16)
    lo = (a - hi.astype(F32)).astype(BF16)
    return hi, lo


def _sigmoid(x):
    return 1.0 / (1.0 + jnp.exp(-x))


def _silu(x):
    return x * _sigmoid(x)


def _inproj_kernel(x_ref, nw_ref, w_ref, wba_ref, p_ref, ba_ref):
    tm = x_ref.shape[0]
    nsub = SUBTILES if tm % (SUBTILES * 2 * SUBLANES) == 0 else 1
    rows = [slice(i * (tm // nsub), (i + 1) * (tm // nsub)) for i in range(nsub)]
    hs = []
    for r in rows:
        x = x_ref[r, :]
        ms = jnp.mean(x * x, axis=-1, keepdims=True)
        hs.append(_split(x * lax.rsqrt(ms + EPS) * nw_ref[...]))
    for r, (hh, hl) in zip(rows, hs):
        p_ref[r, :] = jnp.dot(hh, w_ref[...], preferred_element_type=F32).astype(p_ref.dtype)
        t = jnp.dot(hh, wba_ref[...], preferred_element_type=F32)
        t2 = jnp.dot(hl, wba_ref[:, :LANES], preferred_element_type=F32)
        ba_ref[r, :] = t[:, :LANES] + (t[:, LANES:] + t2)


def _inproj(x2, norm_w, w_main, w_ba, *, tm):
    n, d = x2.shape
    cols = w_main.shape[1]
    return pl.pallas_call(
        _inproj_kernel,
        grid=(n // tm,),
        in_specs=[
            pl.BlockSpec((tm, d), lambda i: (i, 0)),
            pl.BlockSpec((1, d), lambda i: (0, 0)),
            pl.BlockSpec((d, cols), lambda i: (0, 0), pipeline_mode=pl.Buffered(1)),
            pl.BlockSpec((d, 2 * LANES), lambda i: (0, 0)),
        ],
        out_specs=[
            pl.BlockSpec((tm, cols), lambda i: (i, 0)),
            pl.BlockSpec((tm, LANES), lambda i: (i, 0)),
        ],
        out_shape=[
            jax.ShapeDtypeStruct((n, cols), BF16),
            jax.ShapeDtypeStruct((n, LANES), F32),
        ],
        compiler_params=pltpu.CompilerParams(
            dimension_semantics=("arbitrary",),
            vmem_limit_bytes=VMEM_LIMIT_BYTES),
        name="inproj",
    )(x2, norm_w, w_main, w_ba)


def _delta_kernel(qkv_ref, z_ref, ba_ref, cw_ref, alog_ref, dtb_ref, nw_ref, wo_ref, out_ref,
                  xh_ref, act_ref, g_ref, gt_ref, s_ref, o_ref, *, tt):
    width = HEADS * HEAD_DIM
    nchunk = tt // CHUNK
    nslab = 3 * HEADS
    hist = SUBLANES
    t = pl.program_id(1)

    @pl.when(t == 0)
    def _():
        xh_ref[:, 0:hist, :] = jnp.zeros((nslab, hist, HEAD_DIM), F32)
        s_ref[...] = jnp.zeros_like(s_ref)

    for j in range(nslab):
        xh_ref[j, hist:hist + tt, :] = qkv_ref[:, j * HEAD_DIM:(j + 1) * HEAD_DIM].astype(F32)

    rb = 4 * SUBLANES

    def conv_slab(j, post):
        for r0 in range(0, tt, rb):
            acc = None
            for k in range(DN_CONV):
                lo = r0 + hist - (DN_CONV - 1) + k
                term = xh_ref[j, lo:lo + rb, :] * cw_ref[j, k:k + 1, :]
                acc = term if acc is None else acc + term
            y = _silu(acc)
            if post is not None:
                y = y * (lax.rsqrt(jnp.sum(y * y, axis=-1, keepdims=True) + EPS) * post)
            act_ref[j, r0:r0 + rb, :] = y

    def qk_body(j, carry):
        conv_slab(j, jnp.where(j < HEADS, HEAD_DIM ** -0.5, 1.0).astype(F32))
        return carry

    def v_body(j, carry):
        conv_slab(j, None)
        return carry

    lax.fori_loop(0, 2 * HEADS, qk_body, 0, unroll=4)
    lax.fori_loop(2 * HEADS, nslab, v_body, 0, unroll=4)
    xh_ref[:, 0:hist, :] = xh_ref[:, tt:tt + hist, :]

    ba = ba_ref[...]
    lane = lax.broadcasted_iota(jnp.int32, (tt, LANES), 1)
    beta = _sigmoid(ba)
    sp_in = ba + dtb_ref[...]
    softplus = jnp.maximum(sp_in, 0.0) + jnp.log1p(jnp.exp(-jnp.abs(sp_in)))
    g = -jnp.exp(alog_ref[...]) * softplus
    g = jnp.where((lane >= HEADS) & (lane < 2 * HEADS), g, 0.0)
    row = lax.broadcasted_iota(jnp.int32, (tt, tt), 0)
    col = lax.broadcasted_iota(jnp.int32, (tt, tt), 1)
    tri = jnp.where((row // CHUNK == col // CHUNK) & (row >= col), 1.0, 0.0).astype(BF16)
    gh, gl = _split(g)
    gc = jnp.dot(tri, gh, preferred_element_type=F32) + jnp.dot(tri, gl, preferred_element_type=F32)
    gfull = jnp.where(lane < HEADS, beta, gc)
    g_ref[...] = gfull
    gt = gfull.T
    for c in range(nchunk):
        gt_ref[c] = gt[:, c * CHUNK:(c + 1) * CHUNK]

    ri = lax.broadcasted_iota(jnp.int32, (CHUNK, CHUNK), 0)
    ci = lax.broadcasted_iota(jnp.int32, (CHUNK, CHUNK), 1)
    tril = ri >= ci
    eye = jnp.where(ri == ci, 1.0, 0.0).astype(F32)
    level_masks = []
    s = 1
    while s < CHUNK:
        level_masks.append((ri // (2 * s) == ci // (2 * s)) & ((ri // s) % 2 == 1) & ((ci // s) % 2 == 0))
        s *= 2

    cpi = min(CHUNKS_PER_ITER, nchunk)

    def chunk_body(ci, carry):
        ps = [(cc, h) for cc in range(cpi) for h in range(HEADS)]
        r0 = [pl.multiple_of((ci * cpi + cc) * CHUNK, CHUNK) for cc in range(cpi)]
        gcb = [g_ref[pl.ds(r0[cc], CHUNK), :] for cc in range(cpi)]
        gtb = [gt_ref[ci * cpi + cc] for cc in range(cpi)]
        beta_c = [gcb[cc][:, h:h + 1] for cc, h in ps]
        gc_c = [gcb[cc][:, HEADS + h:HEADS + h + 1] for cc, h in ps]
        gc_r = [gtb[cc][HEADS + h:HEADS + h + 1, :] for cc, h in ps]
        q = [act_ref[h, pl.ds(r0[cc], CHUNK), :] for cc, h in ps]
        k = [act_ref[HEADS + h, pl.ds(r0[cc], CHUNK), :] for cc, h in ps]
        v = [act_ref[2 * HEADS + h, pl.ds(r0[cc], CHUNK), :] for cc, h in ps]
        n = range(len(ps))

        decay = [jnp.exp(jnp.where(tril, gc_c[i] - gc_r[i], -jnp.inf)) for i in n]
        kq = [_mm_nt(jnp.concatenate([k[i], q[i]], axis=0), k[i]) for i in n]
        m = [beta_c[i] * kq[i][:CHUNK] * decay[i] for i in n]
        attn = [kq[i][CHUNK:] * decay[i] for i in n]

        x = [eye - jnp.where(level_masks[0], m[i], 0.0) for i in n]
        for lm in level_masks[1:]:
            pm = [_mm(jnp.where(lm, m[i], 0.0), x[i]) for i in n]
            x = [x[i] - _mm(x[i], pm[i]) for i in n]

        egc = [jnp.exp(gc_c[i]) for i in n]
        rhs = [jnp.concatenate([v[i] * beta_c[i], k[i] * (beta_c[i] * egc[i])], axis=1) for i in n]
        uw = [_mm(x[i], rhs[i]) for i in n]

        g_last = [gc_c[i][CHUNK - 1:CHUNK, :] for i in n]
        k_dec = [k[i] * jnp.exp(g_last[i] - gc_c[i]) for i in n]
        u = [uw[i][:, :HEAD_DIM].astype(BF16) for i in n]
        w = [uw[i][:, HEAD_DIM:] for i in n]
        gmat = [_mm_tn(k_dec[i], w[i]) for i in n]
        aw = [_mm(attn[i], w[i]) for i in n]
        lhs = [jnp.concatenate([
            jnp.concatenate([(q[i] * egc[i] - aw[i]).astype(BF16), attn[i].astype(BF16)], axis=1),
            jnp.concatenate([(-gmat[i]).astype(BF16), k_dec[i].T.astype(BF16)], axis=1)], axis=0) for i in n]
        st = [s_ref[h] for h in range(HEADS)]
        for cc in range(cpi):
            idx = [cc * HEADS + h for h in range(HEADS)]
            r = [jnp.dot(lhs[i], jnp.concatenate([st[h].astype(BF16), u[i]], axis=0),
                         preferred_element_type=F32) for h, i in enumerate(idx)]
            for h, i in enumerate(idx):
                o_ref[pl.ds(r0[cc], CHUNK), h * HEAD_DIM:(h + 1) * HEAD_DIM] = r[h][:CHUNK]
            st = [st[h] * jnp.exp(g_last[i]) + r[h][CHUNK:] for h, i in enumerate(idx)]
        for h in range(HEADS):
            s_ref[h] = st[h]
        return carry

    lax.fori_loop(0, nchunk // cpi, chunk_body, 0)

    parts = []
    for h in range(HEADS):
        o = o_ref[:, h * HEAD_DIM:(h + 1) * HEAD_DIM]
        z = z_ref[:, h * HEAD_DIM:(h + 1) * HEAD_DIM].astype(F32)
        on = o * lax.rsqrt(jnp.mean(o * o, axis=-1, keepdims=True) + EPS) * nw_ref[...]
        parts.append((on * _silu(z)).astype(BF16))
    og = jnp.concatenate(parts, axis=1)
    out_ref[...] = jnp.dot(og, wo_ref[...], preferred_element_type=F32).astype(out_ref.dtype)


def _delta_branch(p, ba, conv_w, alog, dtb, norm_w, w_o, *, batch, seq, tt):
    width = HEADS * HEAD_DIM
    nt = seq // tt
    d_out = w_o.shape[1]
    kern = functools.partial(_delta_kernel, tt=tt)
    row = lambda b, t: b * nt + t
    return pl.pallas_call(
        kern,
        grid=(batch, nt),
        in_specs=[
            pl.BlockSpec((tt, 3 * width), lambda b, t: (row(b, t), 0)),
            pl.BlockSpec((tt, width), lambda b, t: (row(b, t), 3)),
            pl.BlockSpec((tt, LANES), lambda b, t: (row(b, t), 0)),
            pl.BlockSpec((3 * HEADS, SUBLANES, HEAD_DIM), lambda b, t: (0, 0, 0)),
            pl.BlockSpec((1, LANES), lambda b, t: (0, 0)),
            pl.BlockSpec((1, LANES), lambda b, t: (0, 0)),
            pl.BlockSpec((1, HEAD_DIM), lambda b, t: (0, 0)),
            pl.BlockSpec((width, d_out), lambda b, t: (0, 0)),
        ],
        out_specs=pl.BlockSpec((tt, d_out), lambda b, t: (row(b, t), 0)),
        out_shape=jax.ShapeDtypeStruct((batch * seq, d_out), BF16),
        scratch_shapes=[
            pltpu.VMEM((3 * HEADS, SUBLANES + tt, HEAD_DIM), F32),
            pltpu.VMEM((3 * HEADS, tt, HEAD_DIM), F32),
            pltpu.VMEM((tt, LANES), F32),
            pltpu.VMEM((tt // CHUNK, LANES, CHUNK), F32),
            pltpu.VMEM((HEADS, HEAD_DIM, HEAD_DIM), F32),
            pltpu.VMEM((tt, width), F32),
        ],
        compiler_params=pltpu.CompilerParams(
            dimension_semantics=("arbitrary", "arbitrary"),
            vmem_limit_bytes=VMEM_LIMIT_BYTES),
        name="delta_branch",
    )(p, p, ba, conv_w, alog, dtb, norm_w, w_o)


def _convmod_kernel(glu_ref, bglu_ref, dw_ref, dwb_ref, lnw_ref, lnb_ref, w2_ref, b2_ref, out_ref,
                    x0_ref, x1_ref, xf_ref, y_ref, c_ref, *, tt):
    width = lnw_ref.shape[1]
    nslab = width // LANES
    hw = 2 * SUBLANES
    rb = 2 * SUBLANES
    group = 4
    t = pl.program_id(1)

    @pl.when(t == 0)
    def _():
        x0_ref[:, 0:hw, :] = jnp.zeros((nslab, hw, LANES), jnp.uint32)
        x1_ref[:, 0:hw, :] = jnp.zeros((nslab, hw, LANES), jnp.uint32)
        xf_ref[:, 0:SUBLANES, :] = jnp.zeros((nslab, SUBLANES, LANES), F32)

    for j in range(nslab):
        ca = slice(j * LANES, (j + 1) * LANES)
        cb = slice(width + j * LANES, width + (j + 1) * LANES)
        ga = glu_ref[:, ca].astype(F32) + bglu_ref[:, ca]
        gb = glu_ref[:, cb].astype(F32) + bglu_ref[:, cb]
        c = ga * _sigmoid(gb)
        xf_ref[j, SUBLANES:SUBLANES + tt, :] = c
        x0_ref[j, hw:hw + tt // 2, :] = pltpu.bitcast(c.astype(BF16), jnp.uint32)
        shifted = xf_ref[j, SUBLANES - 1:SUBLANES - 1 + tt, :]
        x1_ref[j, hw:hw + tt // 2, :] = pltpu.bitcast(shifted.astype(BF16), jnp.uint32)

    def conv_body(j, carry):
        for g0 in range(0, tt, group * rb):
            acc = [None] * group
            for k in range(CM_KERNEL):
                wk = dw_ref[j, k].astype(F32)
                for b in range(group):
                    off = g0 + b * rb - (CM_KERNEL - 1) + k
                    if off % 2 == 0:
                        words = x0_ref[j, hw + off // 2:hw + off // 2 + rb // 2, :]
                    else:
                        words = x1_ref[j, hw + (off + 1) // 2:hw + (off + 1) // 2 + rb // 2, :]
                    term = pltpu.bitcast(words, BF16).astype(F32) * wk
                    acc[b] = term if acc[b] is None else acc[b] + term
            for b in range(group):
                r0 = g0 + b * rb
                y_ref[j, r0:r0 + rb, :] = acc[b] + dwb_ref[j]
        return carry

    lax.fori_loop(0, nslab, conv_body, 0)
    x0_ref[:, 0:hw, :] = x0_ref[:, tt // 2:tt // 2 + hw, :]
    x1_ref[:, 0:hw, :] = x1_ref[:, tt // 2:tt // 2 + hw, :]
    xf_ref[:, 0:SUBLANES, :] = xf_ref[:, tt:tt + SUBLANES, :]

    tot = y_ref[0]
    for j in range(1, nslab):
        tot = tot + y_ref[j]
    mu = jnp.sum(tot, axis=-1, keepdims=True) * (1.0 / width)
    sq = None
    for j in range(nslab):
        cc = y_ref[j] - mu
        sq = cc * cc if sq is None else sq + cc * cc
    rstd = lax.rsqrt(jnp.sum(sq, axis=-1, keepdims=True) * (1.0 / width) + EPS)
    for j in range(nslab):
        ca = slice(j * LANES, (j + 1) * LANES)
        y = (y_ref[j] - mu) * rstd * lnw_ref[:, ca] + lnb_ref[:, ca]
        c_ref[:, ca] = _silu(y).astype(BF16)

    out = jnp.dot(c_ref[...], w2_ref[...], preferred_element_type=F32) + b2_ref[...]
    out_ref[...] = out.astype(out_ref.dtype)


def _conv_branch(p, b_glu, dw_taps, dw_bias, ln_w, ln_b, w_pw2, b_pw2, *, batch, seq, tt):
    width = ln_w.shape[1]
    nslab = width // LANES
    d_out = w_pw2.shape[1]
    nt = seq // tt
    kern = functools.partial(_convmod_kernel, tt=tt)
    row = lambda b, t: b * nt + t
    const = lambda b, t: (0, 0)
    return pl.pallas_call(
        kern,
        grid=(batch, nt),
        in_specs=[
            pl.BlockSpec((tt, 2 * width), lambda b, t: (row(b, t), 2)),
            pl.BlockSpec((1, 2 * width), const),
            pl.BlockSpec((nslab, CM_KERNEL, 2 * SUBLANES, LANES), lambda b, t: (0, 0, 0, 0)),
            pl.BlockSpec((nslab, 1, LANES), lambda b, t: (0, 0, 0)),
            pl.BlockSpec((1, width), const),
            pl.BlockSpec((1, width), const),
            pl.BlockSpec((width, d_out), const),
            pl.BlockSpec((1, d_out), const),
        ],
        out_specs=pl.BlockSpec((tt, d_out), lambda b, t: (row(b, t), 0)),
        out_shape=jax.ShapeDtypeStruct((batch * seq, d_out), BF16),
        scratch_shapes=[
            pltpu.VMEM((nslab, 2 * SUBLANES + tt // 2, LANES), jnp.uint32),
            pltpu.VMEM((nslab, 2 * SUBLANES + tt // 2, LANES), jnp.uint32),
            pltpu.VMEM((nslab, SUBLANES + tt, LANES), F32),
            pltpu.VMEM((nslab, tt, LANES), F32),
            pltpu.VMEM((tt, width), BF16),
        ],
        compiler_params=pltpu.CompilerParams(
            dimension_semantics=("arbitrary", "arbitrary"),
            vmem_limit_bytes=VMEM_LIMIT_BYTES),
        name="conv_branch",
    )(p, b_glu, dw_taps, dw_bias, ln_w, ln_b, w_pw2, b_pw2)


def _merge_ffn_kernel(x_ref, oa_ref, ob_ref, gates_ref, bg_ref, wout_ref, n2_ref, wgu_ref, wdn_ref,
                      nf_ref, out_ref):
    d = x_ref.shape[1]
    hidden = wdn_ref.shape[0]
    tm = x_ref.shape[0]
    nsub = SUBTILES if tm % (SUBTILES * 2 * SUBLANES) == 0 else 1
    rows = [slice(i * (tm // nsub), (i + 1) * (tm // nsub)) for i in range(nsub)]
    merged = []
    for r in rows:
        gate_a = _sigmoid(gates_ref[r, :d].astype(F32) + bg_ref[:, :d])
        gate_b = _sigmoid(gates_ref[r, d:].astype(F32) + bg_ref[:, d:])
        merged.append((gate_a * oa_ref[r, :].astype(F32) + gate_b * ob_ref[r, :].astype(F32)).astype(BF16))
    x1 = [x_ref[r, :] + jnp.dot(m, wout_ref[...], preferred_element_type=F32) for r, m in zip(rows, merged)]
    h2 = [(v * lax.rsqrt(jnp.mean(v * v, axis=-1, keepdims=True) + EPS) * n2_ref[...]).astype(BF16) for v in x1]
    gu = [jnp.dot(h, wgu_ref[...], preferred_element_type=F32) for h in h2]
    act = [(_silu(g[:, :hidden]) * g[:, hidden:]).astype(BF16) for g in gu]
    x2 = [v + jnp.dot(a, wdn_ref[...], preferred_element_type=F32) for v, a in zip(x1, act)]
    for r, v in zip(rows, x2):
        out_ref[r, :] = v * lax.rsqrt(jnp.mean(v * v, axis=-1, keepdims=True) + EPS) * nf_ref[...]


def _merge_ffn(x2d, out_a, out_b, p, b_gates, w_out, norm2_w, w_gu, w_dn, norm_f_w, *, tm):
    n, d = x2d.shape
    hidden = w_dn.shape[0]
    const = lambda i: (0, 0)
    single = dict(pipeline_mode=pl.Buffered(1))
    return pl.pallas_call(
        _merge_ffn_kernel,
        grid=(n // tm,),
        in_specs=[
            pl.BlockSpec((tm, d), lambda i: (i, 0)),
            pl.BlockSpec((tm, d), lambda i: (i, 0)),
            pl.BlockSpec((tm, d), lambda i: (i, 0)),
            pl.BlockSpec((tm, 2 * d), lambda i: (i, 3)),
            pl.BlockSpec((1, 2 * d), const),
            pl.BlockSpec((d, d), const, **single),
            pl.BlockSpec((1, d), const),
            pl.BlockSpec((d, 2 * hidden), const, **single),
            pl.BlockSpec((hidden, d), const, **single),
            pl.BlockSpec((1, d), const),
        ],
        out_specs=pl.BlockSpec((tm, d), lambda i: (i, 0)),
        out_shape=jax.ShapeDtypeStruct((n, d), F32),
        compiler_params=pltpu.CompilerParams(
            dimension_semantics=("arbitrary",),
            vmem_limit_bytes=VMEM_LIMIT_BYTES),
        name="merge_ffn",
    )(x2d, out_a, out_b, p, b_gates, w_out, norm2_w, w_gu, w_dn, norm_f_w)


def _pick_tile(n, target):
    t = min(n, target)
    while n % t:
        t //= 2
    return t


def kernel(x, norm1_w, w_in, b_glu, b_gates, dn_conv_w, dn_A_log, dn_dt_bias, dn_norm_w, dn_w_o,
           cm_dw_w, cm_dw_b, cm_ln_w, cm_ln_b, cm_w_pw2, cm_b_pw2, w_out, norm2_w, ffn_w_gate_up,
           ffn_w_down, norm_f_w):
    batch, seq, d = x.shape
    depth = w_in.shape[0]
    width = HEADS * HEAD_DIM
    assert d == width and cm_ln_w.shape[1] == d and seq % CHUNK == 0
    n = batch * seq
    tt = _pick_tile(seq, 512)
    xf = x.reshape(n, d)

    for l in range(depth):
        wl = w_in[l]
        wl_bf = wl.astype(BF16)
        w_main = jnp.concatenate([wl_bf[:, :4 * width], wl_bf[:, 4 * width + 2 * HEADS:]], axis=1)
        w_ba = jnp.pad(wl[:, 4 * width:4 * width + 2 * HEADS], ((0, 0), (0, LANES - 2 * HEADS)))
        w_ba_hi = w_ba.astype(BF16)
        w_ba_lo = (w_ba - w_ba_hi.astype(F32)).astype(BF16)
        w_ba2 = jnp.concatenate([w_ba_hi, w_ba_lo], axis=1)
        lane_pad = lambda v: jnp.pad(v.reshape(1, HEADS), ((0, 0), (HEADS, LANES - 2 * HEADS)))

        p, ba = _inproj(xf, norm1_w[l].reshape(1, d), w_main, w_ba2,
                        tm=_pick_tile(n, 512))
        dn_cw = jnp.pad(dn_conv_w[l].reshape(DN_CONV, 3 * HEADS, HEAD_DIM).transpose(1, 0, 2),
                        ((0, 0), (0, SUBLANES - DN_CONV), (0, 0)))
        cm_taps = cm_dw_w[l].reshape(CM_KERNEL, d // LANES, LANES).transpose(1, 0, 2).astype(BF16)
        cm_taps = jnp.broadcast_to(cm_taps[:, :, None, :], (d // LANES, CM_KERNEL, 2 * SUBLANES, LANES))
        cm_bias = cm_dw_b[l].reshape(d // LANES, 1, LANES)

        out_a = _delta_branch(p, ba, dn_cw, lane_pad(dn_A_log[l]), lane_pad(dn_dt_bias[l]),
                              dn_norm_w[l].reshape(1, HEAD_DIM), dn_w_o[l].astype(BF16),
                              batch=batch, seq=seq, tt=tt)
        out_b = _conv_branch(p, b_glu[l].reshape(1, 2 * d), cm_taps, cm_bias,
                             cm_ln_w[l].reshape(1, d), cm_ln_b[l].reshape(1, d),
                             cm_w_pw2[l].astype(BF16), cm_b_pw2[l].reshape(1, d),
                             batch=batch, seq=seq, tt=tt)
        assert depth == 1
        xf = _merge_ffn(xf, out_a, out_b, p, b_gates[l].reshape(1, 2 * d), w_out[l].astype(BF16),
                        norm2_w[l].reshape(1, d), ffn_w_gate_up[l].astype(BF16),
                        ffn_w_down[l].astype(BF16), norm_f_w.reshape(1, d), tm=_pick_tile(n, 512))
    return xf.reshape(batch, seq, d)
```
